```python
import jax, jax.numpy as jnp
from jax import lax
import numpy as np

D_MODEL = 1024
BATCH = 8
SEQ = 2048
DEPTH = 4

N_MIXERS = 3
EPS = 1e-6
N_NORMS = 6
D_FF = 2816
CONV_W = 3
ATT_HEADS = 16
ATT_HD = 64
DILATION_PAIRS = ((128, 1), (512, 4), (2048, 16))
N_GROUPS = len(DILATION_PAIRS)
ATT_BLOCK = 128
ROPE_THETA = 500000.0
ROT_DIM = ATT_HD // 4
ML_HEADS = 4
ML_DQK = 128
ML_DV = D_MODEL // ML_HEADS
ML_CHUNK = 64
ML_PROJ = 2 * ML_HEADS * ML_DQK + 2 * ML_HEADS * ML_DV + 2 * ML_HEADS
N_CONV_LAYERS = (DEPTH + 2) // 3
N_ATTN_LAYERS = (DEPTH + 1) // 3
N_MLSTM_LAYERS = DEPTH // 3

kernel_name = "hybrid_conv_dilattn_mlstm_macaron"


def rmsnorm(x, g):
    x32 = x.astype(jnp.float32)
    y = x32 * lax.rsqrt(jnp.mean(x32 * x32, axis=-1, keepdims=True) + EPS)
    return (y * g.astype(jnp.float32)).astype(x.dtype)


def swiglu(x, wg, wu, wd):
    return (jax.nn.silu(x @ wg) * (x @ wu)) @ wd


def short_conv_mixer(x, w_in, k, w_out):
    S = x.shape[1]
    b, c, u = jnp.split(x @ w_in, 3, axis=-1)
    u = c * u
    up = jnp.pad(u, ((0, 0), (CONV_W - 1, 0), (0, 0)))
    conv = sum(k[j] * up[:, CONV_W - 1 - j:CONV_W - 1 - j + S] for j in range(CONV_W))
    return (b * conv) @ w_out


def rope_tables(S):
    pos = jnp.arange(S, dtype=jnp.float32)
    inv = ROPE_THETA ** (-jnp.arange(0, ROT_DIM, 2, dtype=jnp.float32) / ROT_DIM)
    ang = pos[:, None] * inv[None, :]
    return jnp.cos(ang), jnp.sin(ang)


def apply_partial_rope(x, cos, sin):
    half = ROT_DIM // 2
    x1, x2, rest = x[..., :half], x[..., half:ROT_DIM], x[..., ROT_DIM:]
    c, s = cos[None, :, None, :], sin[None, :, None, :]
    return jnp.concatenate([x1 * c - x2 * s, x2 * c + x1 * s, rest], axis=-1).astype(x.dtype)


def dilated_window_attention(q, k, v, win, dil):
    B, S, H, hd = q.shape
    L = S // dil
    nb = -(-L // ATT_BLOCK)
    Lp = nb * ATT_BLOCK

    def to_sub(t):
        t = t.reshape(B, L, dil, H, hd).transpose(0, 2, 3, 1, 4)
        t = jnp.pad(t, ((0, 0), (0, 0), (0, 0), (0, Lp - L), (0, 0)))
        return t.reshape(B, dil, H, nb, ATT_BLOCK, hd)

    def with_prev(t):
        prev = jnp.pad(t, ((0, 0), (0, 0), (0, 0), (1, 0), (0, 0), (0, 0)))[:, :, :, :-1]
        return jnp.concatenate([prev, t], axis=-2)

    qb = to_sub(q)
    kk = with_prev(to_sub(k))
    vv = with_prev(to_sub(v)).astype(jnp.float32)
    scores = jnp.einsum('brhnqd,brhnkd->brhnqk', qb, kk,
                        preferred_element_type=jnp.float32) * (hd ** -0.5)
    qi = jnp.arange(ATT_BLOCK)[:, None] + ATT_BLOCK
    kj = jnp.arange(2 * ATT_BLOCK)[None, :]
    rel = qi - kj
    band = (rel >= 0) & (rel <= win)
    key_ok = (jnp.arange(nb)[:, None] * ATT_BLOCK - ATT_BLOCK + kj) >= 0
    mask = band[None] & key_ok[:, None, :]
    scores = jnp.where(mask, scores, -jnp.inf)
    m = jnp.max(scores, axis=-1, keepdims=True)
    p = jnp.exp(scores - m)
    den = jnp.sum(p, axis=-1)
    out = jnp.einsum('brhnqk,brhnkd->brhnqd', p, vv) / den[..., None]
    lse = m[..., 0] + jnp.log(den)
    out = out.reshape(B, dil, H, Lp, hd)[:, :, :, :L].transpose(0, 3, 1, 2, 4).reshape(B, S, H, hd)
    lse = lse.reshape(B, dil, H, Lp)[..., :L].transpose(0, 3, 1, 2).reshape(B, S, H)
    return out, lse


def dilated_attention_mixer(x, w_in, w_out, cos, sin):
    B, S, _ = x.shape
    qkv = (x @ w_in).reshape(B, S, N_GROUPS, 3, ATT_HEADS, ATT_HD)
    outs, lses = [], []
    for g, (win, dil) in enumerate(DILATION_PAIRS):
        q = apply_partial_rope(qkv[:, :, g, 0], cos, sin)
        k = apply_partial_rope(qkv[:, :, g, 1], cos, sin)
        v = qkv[:, :, g, 2]
        o, l = dilated_window_attention(q, k, v, win // dil, dil)
        outs.append(o)
        lses.append(l)
    alpha = jax.nn.softmax(jnp.stack(lses, axis=0), axis=0)
    o = jnp.sum(alpha[..., None] * jnp.stack(outs, axis=0), axis=0)
    return o.reshape(B, S, ATT_HEADS * ATT_HD).astype(x.dtype) @ w_out


def mlstm_mixer(x, w_in, b_i, b_f, w_out):
    B, S, _ = x.shape
    NH, L = ML_HEADS, ML_CHUNK
    nc = S // L
    p = x @ w_in
    o1 = NH * ML_DQK
    o2 = 2 * o1
    o3 = o2 + NH * ML_DV
    o4 = o3 + NH * ML_DV
    o5 = o4 + NH
    q = p[..., :o1].reshape(B, S, NH, ML_DQK).astype(jnp.float32)
    k = p[..., o1:o2].reshape(B, S, NH, ML_DQK).astype(jnp.float32) * (ML_DQK ** -0.5)
    v = p[..., o2:o3].reshape(B, S, NH, ML_DV).astype(jnp.float32)
    og = jax.nn.sigmoid(p[..., o3:o4].astype(jnp.float32))
    ig = (p[..., o4:o5] + b_i).astype(jnp.float32)
    lf = jax.nn.log_sigmoid((p[..., o5:] + b_f).astype(jnp.float32))

    def vec_chunks(t):
        return t.reshape(B, nc, L, NH, -1).transpose(1, 0, 3, 2, 4)

    def gate_chunks(t):
        return t.reshape(B, nc, L, NH).transpose(1, 0, 3, 2)

    tril = jnp.tril(jnp.ones((L, L), dtype=bool))

    def step(carry, inp):
        C, n, m = carry
        qc, kc, vc, ic, fc = inp
        b = jnp.cumsum(fc, axis=-1)
        dlog = jnp.where(tril, b[..., :, None] - b[..., None, :] + ic[..., None, :], -jnp.inf)
        inter = b + m[..., None]
        mt = jnp.maximum(inter, jnp.max(dlog, axis=-1))
        w = jnp.exp(dlog - mt[..., None])
        wi = jnp.exp(inter - mt)
        a = jnp.einsum('bhtd,bhsd->bhts', qc, kc) * w
        num = jnp.einsum('bhts,bhsv->bhtv', a, vc) + wi[..., None] * jnp.einsum('bhvd,bhtd->bhtv', C, qc)
        den = jnp.sum(a, axis=-1) + wi * jnp.einsum('bhd,bhtd->bht', n, qc)
        h = num / jnp.maximum(jnp.abs(den), jnp.exp(-mt))[..., None]
        m_new = mt[..., -1]
        we = jnp.exp(b[..., -1:] - b + ic - m_new[..., None])
        decay = jnp.exp(b[..., -1] + m - m_new)
        C = decay[..., None, None] * C + jnp.einsum('bhs,bhsv,bhsd->bhvd', we, vc, kc)
        n = decay[..., None] * n + jnp.einsum('bhs,bhsd->bhd', we, kc)
        return (C, n, m_new), h

    init = (jnp.zeros((B, NH, ML_DV, ML_DQK), jnp.float32),
            jnp.zeros((B, NH, ML_DQK), jnp.float32),
            jnp.zeros((B, NH), jnp.float32))
    _, h = lax.scan(step, init, (vec_chunks(q), vec_chunks(k), vec_chunks(v),
                                 gate_chunks(ig), gate_chunks(lf)))
    h = h.transpose(1, 0, 3, 2, 4).reshape(B, S, NH * ML_DV)
    return (h * og).astype(x.dtype) @ w_out


def setup_inputs(seed: int = 0) -> dict:
    key = jax.random.key(seed)
    ks = jax.random.split(key, 18)
    D = D_MODEL

    def w(k, shape, fan_in):
        return jax.random.normal(k, shape, jnp.float32) * fan_in ** -0.5

    return {
        "x": jax.random.normal(ks[0], (BATCH, SEQ, D), jnp.float32),
        "norm_g": 1.0 + 0.05 * jax.random.normal(ks[1], (DEPTH, N_NORMS, D), jnp.float32),
        "ffn1_wg": w(ks[2], (DEPTH, D, D_FF), D),
        "ffn1_wu": w(ks[3], (DEPTH, D, D_FF), D),
        "ffn1_wd": w(ks[4], (DEPTH, D_FF, D), D_FF),
        "ffn2_wg": w(ks[5], (DEPTH, D, D_FF), D),
        "ffn2_wu": w(ks[6], (DEPTH, D, D_FF), D),
        "ffn2_wd": w(ks[7], (DEPTH, D_FF, D), D_FF),
        "conv_w_in": w(ks[8], (N_CONV_LAYERS, D, 3 * D), D),
        "conv_k": w(ks[9], (N_CONV_LAYERS, CONV_W, D), CONV_W),
        "conv_w_out": w(ks[10], (N_CONV_LAYERS, D, D), D),
        "attn_w_in": w(ks[11], (N_ATTN_LAYERS, D, N_GROUPS * 3 * ATT_HEADS * ATT_HD), D),
        "attn_w_out": w(ks[12], (N_ATTN_LAYERS, ATT_HEADS * ATT_HD, D), ATT_HEADS * ATT_HD),
        "mlstm_w_in": w(ks[13], (N_MLSTM_LAYERS, D, ML_PROJ), D),
        "mlstm_b_i": 0.01 * jax.random.normal(ks[14], (N_MLSTM_LAYERS, ML_HEADS), jnp.float32),
        "mlstm_b_f": jnp.linspace(3.0, 6.0, ML_HEADS, dtype=jnp.float32)[None, :]
                     + 0.01 * jax.random.normal(ks[15], (N_MLSTM_LAYERS, ML_HEADS), jnp.float32),
        "mlstm_w_out": w(ks[16], (N_MLSTM_LAYERS, ML_HEADS * ML_DV, D), ML_HEADS * ML_DV),
    }


def reference(x, norm_g, ffn1_wg, ffn1_wu, ffn1_wd, ffn2_wg, ffn2_wu, ffn2_wd,
              conv_w_in, conv_k, conv_w_out, attn_w_in, attn_w_out,
              mlstm_w_in, mlstm_b_i, mlstm_b_f, mlstm_w_out):
    cos, sin = rope_tables(x.shape[1])
    for i in range(DEPTH):
        g = norm_g[i]
        x = x + 0.5 * rmsnorm(swiglu(rmsnorm(x, g[0]), ffn1_wg[i], ffn1_wu[i], ffn1_wd[i]), g[1])
        xn = rmsnorm(x, g[2])
        mixer, j = i % N_MIXERS, i // N_MIXERS
        if mixer == 0:
            y = short_conv_mixer(xn, conv_w_in[j], conv_k[j], conv_w_out[j])
        elif mixer == 1:
            y = dilated_attention_mixer(xn, attn_w_in[j], attn_w_out[j], cos, sin)
        else:
            y = mlstm_mixer(xn, mlstm_w_in[j], mlstm_b_i[j], mlstm_b_f[j], mlstm_w_out[j])
        x = x + rmsnorm(y, g[3])
        x = x + 0.5 * rmsnorm(swiglu(rmsnorm(x, g[4]), ffn2_wg[i], ffn2_wu[i], ffn2_wd[i]), g[5])
    return x
```

```python
import functools
import math

import jax
import jax.numpy as jnp
from jax import lax
from jax.experimental import pallas as pl
from jax.experimental.pallas import tpu as pltpu

LANES = 128
SUBLANES = 8
V7X_VMEM_BYTES = 64 * 1024 * 1024

EPS = 1e-6
CONV_W = 3
ATT_HEADS = 16
ATT_HD = 64
DILATION_PAIRS = ((128, 1), (512, 4), (2048, 16))
ATT_BLOCK = 128
ROPE_THETA = 500000.0
ROT_DIM = ATT_HD // 4
ML_HEADS = 4
ML_DQK = 128
ML_DV = 256
ML_CHUNK = 128
N_MIXERS = 3

BF16 = jnp.bfloat16
F32 = jnp.float32
NEG_INF = float("-inf")


def _params(semantics, vmem_bytes):
    return pltpu.CompilerParams(dimension_semantics=semantics,
                                vmem_limit_bytes=min(int(vmem_bytes), V7X_VMEM_BYTES - (4 << 20)))


def _rms(x32, g_row):
    ms = jnp.mean(x32 * x32, axis=-1, keepdims=True)
    return (x32 * lax.rsqrt(ms + EPS)) * g_row


def _dot(a, b):
    return jnp.dot(a, b, preferred_element_type=F32)


def _dot_nt(a, b):
    return lax.dot_general(a, b, (((1,), (1,)), ((), ())), preferred_element_type=F32)


def _dot_f32(a, b):
    return jnp.dot(a, b, preferred_element_type=F32, precision=lax.Precision.HIGHEST)


def _ffn_kernel(x_ref, gpre_ref, gpost_ref, wg_ref, wu_ref, wd_ref, o_ref, xn_ref, acc_ref):
    j = pl.program_id(1)
    nj = pl.num_programs(1)

    @pl.when(j == 0)
    def _():
        xn_ref[...] = _rms(x_ref[...], gpre_ref[...]).astype(BF16)

    xn = xn_ref[...]
    g = _dot(xn, wg_ref[...])
    u = _dot(xn, wu_ref[...])
    h = ((g * jax.nn.sigmoid(g)) * u).astype(BF16)
    part = _dot(h, wd_ref[...])

    @pl.when(j == 0)
    def _():
        acc_ref[...] = part

    @pl.when(j > 0)
    def _():
        acc_ref[...] += part

    @pl.when(j == nj - 1)
    def _():
        o_ref[...] = x_ref[...] + 0.5 * _rms(acc_ref[...], gpost_ref[...])


def _ffn(x, g_pre, g_post, wg, wu, wd, *, tm=512, tf=1408):
    m, d = x.shape
    ff = wg.shape[1]
    tm = min(tm, m)
    tf = min(tf, ff)
    assert m % tm == 0 and ff % tf == 0
    vmem = (4 * tm * d * 4
            + 2 * 3 * d * tf * 2
            + tm * d * (2 + 4)
            + tm * tf * (4 + 4 + 2 + 4)
            + tm * d * 4
            + (4 << 20))
    return pl.pallas_call(
        _ffn_kernel,
        grid=(m // tm, ff // tf),
        in_specs=[
            pl.BlockSpec((tm, d), lambda i, j: (i, 0)),
            pl.BlockSpec((1, d), lambda i, j: (0, 0)),
            pl.BlockSpec((1, d), lambda i, j: (0, 0)),
            pl.BlockSpec((d, tf), lambda i, j: (0, j)),
            pl.BlockSpec((d, tf), lambda i, j: (0, j)),
            pl.BlockSpec((tf, d), lambda i, j: (j, 0)),
        ],
        out_specs=pl.BlockSpec((tm, d), lambda i, j: (i, 0)),
        out_shape=jax.ShapeDtypeStruct((m, d), F32),
        scratch_shapes=[pltpu.VMEM((tm, d), BF16), pltpu.VMEM((tm, d), F32)],
        compiler_params=_params(("parallel", "arbitrary"), vmem),
        name="ffn",
    )(x, g_pre, g_post, wg, wu, wd)


def _outproj_kernel(a_ref, x_ref, w_ref, g_ref, o_ref):
    y = _dot(a_ref[...], w_ref[...])
    o_ref[...] = x_ref[...] + _rms(y, g_ref[...])


def _outproj(a, x, w, g_post, *, tm=512):
    m, d = x.shape
    k = a.shape[1]
    tm = min(tm, m)
    assert m % tm == 0
    vmem = 2 * tm * k * 2 + 4 * tm * d * 4 + 2 * k * d * 2 + 2 * tm * d * 4 + (4 << 20)
    return pl.pallas_call(
        _outproj_kernel,
        grid=(m // tm,),
        in_specs=[
            pl.BlockSpec((tm, k), lambda i: (i, 0)),
            pl.BlockSpec((tm, d), lambda i: (i, 0)),
            pl.BlockSpec((k, d), lambda i: (0, 0)),
            pl.BlockSpec((1, d), lambda i: (0, 0)),
        ],
        out_specs=pl.BlockSpec((tm, d), lambda i: (i, 0)),
        out_shape=jax.ShapeDtypeStruct((m, d), F32),
        compiler_params=_params(("parallel",), vmem),
        name="outproj",
    )(a, x, w, g_post)


def _conv_kernel(x_ref, gpre_ref, gpost_ref, win_ref, k_ref, wout_ref, o_ref, ext_ref, *, tiles_per_seq):
    i = pl.program_id(0)
    tm, d = x_ref.shape
    halo = SUBLANES
    xn = _rms(x_ref[...], gpre_ref[...]).astype(BF16)
    p = _dot(xn, win_ref[...])
    cu = p[:, d:2 * d] * p[:, 2 * d:3 * d]

    @pl.when(i % tiles_per_seq == 0)
    def _():
        ext_ref[0:halo, :] = jnp.zeros((halo, d), F32)

    ext_ref[halo:halo + tm, :] = cu
    kk = k_ref[...]
    conv = (kk[0:1, :] * cu
            + kk[1:2, :] * ext_ref[halo - 1:halo - 1 + tm, :]
            + kk[2:3, :] * ext_ref[halo - 2:halo - 2 + tm, :])
    ext_ref[0:halo, :] = cu[tm - halo:tm, :]
    y = _dot((p[:, 0:d] * conv).astype(BF16), wout_ref[...])
    o_ref[...] = x_ref[...] + _rms(y, gpost_ref[...])


def _conv_mixer(x, g_pre, g_post, w_in, k, w_out, *, seq, tm=512):
    m, d = x.shape
    tm = min(tm, seq)
    assert seq % tm == 0 and m % seq == 0
    vmem = (4 * tm * d * 4 + 2 * (3 * d * d + d * d) * 2 + (tm + SUBLANES) * d * 4
            + tm * 3 * d * 4 + 6 * tm * d * 4 + (4 << 20))
    return pl.pallas_call(
        functools.partial(_conv_kernel, tiles_per_seq=seq // tm),
        grid=(m // tm,),
        in_specs=[
            pl.BlockSpec((tm, d), lambda i: (i, 0)),
            pl.BlockSpec((1, d), lambda i: (0, 0)),
            pl.BlockSpec((1, d), lambda i: (0, 0)),
            pl.BlockSpec((d, 3 * d), lambda i: (0, 0)),
            pl.BlockSpec((CONV_W, d), lambda i: (0, 0)),
            pl.BlockSpec((d, d), lambda i: (0, 0)),
        ],
        out_specs=pl.BlockSpec((tm, d), lambda i: (i, 0)),
        out_shape=jax.ShapeDtypeStruct((m, d), F32),
        scratch_shapes=[pltpu.VMEM((tm + SUBLANES, d), F32)],
        compiler_params=_params(("arbitrary",), vmem),
        name="conv_mixer",
    )(x, g_pre, g_post, w_in, k, w_out)


def _rope_tables(seq, dil, rows):
    length = seq // dil
    u = jnp.arange(rows, dtype=jnp.int32) % length
    pos = (u[None, :] * dil + jnp.arange(dil, dtype=jnp.int32)[:, None]).astype(F32)
    inv = ROPE_THETA ** (-jnp.arange(0, ROT_DIM, 2, dtype=F32) / ROT_DIM)
    ang = pos[:, :, None] * inv[None, None, :]
    cos, sin = jnp.cos(ang), jnp.sin(ang)
    half = ROT_DIM // 2
    dd = jnp.arange(LANES) % ATT_HD
    fi = dd % half
    cos_l = jnp.take(cos, fi, axis=-1)
    sin_l = jnp.take(sin, fi, axis=-1)
    c_tab = jnp.where(dd < ROT_DIM, cos_l, 1.0)
    s_lo = jnp.where(dd < half, -sin_l, 0.0)
    s_hi = jnp.where((dd >= half) & (dd < ROT_DIM), sin_l, 0.0)
    return c_tab.astype(F32), s_lo.astype(F32), s_hi.astype(F32)


def _qkv_kernel(x_ref, g_ref, w_ref, c_ref, slo_ref, shi_ref, o_ref, *, width):
    xn = _rms(x_ref[...], g_ref[...]).astype(BF16)
    p = _dot(xn, w_ref[...])
    c_tab, s_lo, s_hi = c_ref[...], slo_ref[...], shi_ref[...]
    half = ROT_DIM // 2
    scale = ATT_HD ** -0.5
    for part in range(2):
        for cb in range(width // LANES):
            lo = part * width + cb * LANES
            blk = p[:, lo:lo + LANES]
            rot = (blk * c_tab
                   + pltpu.roll(blk, LANES - half, axis=1) * s_lo
                   + pltpu.roll(blk, half, axis=1) * s_hi)
            if part == 0:
                rot = rot * scale
            o_ref[:, lo:lo + LANES] = rot.astype(BF16)
    o_ref[:, 2 * width:3 * width] = p[:, 2 * width:3 * width].astype(BF16)


def _qkv_proj(x, g_pre, w, *, batch, seq, dil, tm=512):
    m, d = x.shape
    width = w.shape[1] // 3
    length = seq // dil
    rows = m // dil
    tm = min(tm, rows)
    assert rows % tm == 0 and (length % tm == 0 or tm % length == 0)
    tab_rows = max(length, tm)
    c_tab, s_lo, s_hi = _rope_tables(seq, dil, tab_rows)
    tab_blocks = tab_rows // tm
    xv = x.reshape(rows, dil * d)
    tab_spec = pl.BlockSpec((None, tm, LANES), lambda i, r: (r, i % tab_blocks, 0))
    vmem = (2 * tm * d * 4 + 2 * d * 3 * width * 2 + 2 * tm * 3 * width * 2 + 6 * tm * LANES * 4
            + tm * 3 * width * 4 + tm * d * 6 + (4 << 20))
    return pl.pallas_call(
        functools.partial(_qkv_kernel, width=width),
        grid=(rows // tm, dil),
        in_specs=[
            pl.BlockSpec((tm, d), lambda i, r: (i, r)),
            pl.BlockSpec((1, d), lambda i, r: (0, 0)),
            pl.BlockSpec((d, 3 * width), lambda i, r: (0, 0)),
            tab_spec, tab_spec, tab_spec,
        ],
        out_specs=pl.BlockSpec((None, tm, 3 * width), lambda i, r: (r, i, 0)),
        out_shape=jax.ShapeDtypeStruct((dil, rows, 3 * width), BF16),
        compiler_params=_params(("parallel", "parallel"), vmem),
        name=f"qkv_proj_d{dil}",
    )(xv, g_pre, w, c_tab, s_lo, s_hi)


def _attn_block(q, kc, vc, kp, vp, prev_ok):
    blk = q.shape[0]
    lane = lax.broadcasted_iota(jnp.int32, (blk, LANES), 1)
    row = lax.broadcasted_iota(jnp.int32, (blk, LANES), 0)
    keep_cur = lane <= row
    hi_head = lane >= ATT_HD
    zero = jnp.zeros_like(q)
    out = None
    lse = None
    for hh in range(2):
        hsel = hi_head if hh == 1 else jnp.logical_not(hi_head)
        qm = jnp.where(hsel, q, zero)
        s_c = jnp.where(keep_cur, _dot_nt(qm, kc), NEG_INF)
        if kp is not None:
            keep_prev = jnp.logical_and(lane >= row, prev_ok)
            s_p = jnp.where(keep_prev, _dot_nt(qm, kp), NEG_INF)
            mx = jnp.max(jnp.maximum(s_c, s_p), axis=1, keepdims=True)
            p_c = jnp.exp(s_c - mx)
            p_p = jnp.exp(s_p - mx)
            den = jnp.sum(p_c + p_p, axis=1, keepdims=True)
            num = (_dot(p_c.astype(BF16), jnp.where(hsel, vc, zero))
                   + _dot(p_p.astype(BF16), jnp.where(hsel, vp, zero)))
        else:
            mx = jnp.max(s_c, axis=1, keepdims=True)
            p_c = jnp.exp(s_c - mx)
            den = jnp.sum(p_c, axis=1, keepdims=True)
            num = _dot(p_c.astype(BF16), jnp.where(hsel, vc, zero))
        o_h = num / den
        l_h = mx + jnp.log(den)
        out = o_h if out is None else out + o_h
        lse = jnp.broadcast_to(l_h, (blk, LANES)) if lse is None else jnp.where(hsel, l_h, lse)
    return out, lse


def _attn_kernel(*refs, dils, seq):
    n_g = len(dils)
    qkv = refs[:3 * n_g]
    o_ref = refs[3 * n_g]
    oacc_ref, lacc_ref = refs[3 * n_g + 1], refs[3 * n_g + 2]
    blk = ATT_BLOCK

    for g, dil in enumerate(dils):
        q_ref, k_ref, v_ref = qkv[3 * g:3 * g + 3]
        nb = seq // dil // blk

        def body(i, carry, q_ref=q_ref, k_ref=k_ref, v_ref=v_ref, nb=nb, dil=dil, g=g):
            r = i // nb
            n = i % nb
            cur = pl.ds(pl.multiple_of(n * blk, blk), blk)
            q = q_ref[r, cur, :]
            kc = k_ref[r, cur, :]
            vc = v_ref[r, cur, :]
            if nb > 1:
                prev = pl.ds(pl.multiple_of(jnp.maximum(n - 1, 0) * blk, blk), blk)
                out, lse = _attn_block(q, kc, vc, k_ref[r, prev, :], v_ref[r, prev, :], n > 0)
            else:
                out, lse = _attn_block(q, kc, vc, None, None, None)
            start = r + dil * n * blk
            if dil == 1:
                dst = pl.ds(pl.multiple_of(start, blk), blk)
            else:
                dst = pl.ds(start, blk, stride=dil)
            oacc_ref[g, dst, :] = out
            lacc_ref[g, dst, :] = lse
            return carry

        lax.fori_loop(0, seq // blk, body, 0)

    rows = 2 * blk

    def mix(t, carry):
        sl = pl.ds(pl.multiple_of(t * rows, rows), rows)
        ls = [lacc_ref[g, sl, :] for g in range(n_g)]
        mx = functools.reduce(jnp.maximum, ls)
        es = [jnp.exp(l - mx) for l in ls]
        inv = 1.0 / functools.reduce(jnp.add, es)
        acc = None
        for g in range(n_g):
            term = (es[g] * inv) * oacc_ref[g, sl, :]
            acc = term if acc is None else acc + term
        o_ref[sl, :] = acc.astype(BF16)
        return carry

    lax.fori_loop(0, seq // rows, mix, 0)


def _attention(qkvs, *, batch, seq, dils, width):
    n_pairs = width // LANES
    args, specs = [], []
    for arr, dil in zip(qkvs, dils):
        length = seq // dil
        a4 = arr.reshape(dil, batch, length, 3 * width)
        for part in range(3):
            args.append(a4)
            specs.append(pl.BlockSpec((dil, None, length, LANES),
                                      lambda b, hp, part=part: (0, b, 0, part * n_pairs + hp)))
    n_g = len(dils)
    vmem = 2 * 3 * n_g * seq * LANES * 2 + 2 * n_g * seq * LANES * 4 + 2 * seq * LANES * 2 + (8 << 20)
    return pl.pallas_call(
        functools.partial(_attn_kernel, dils=tuple(dils), seq=seq),
        grid=(batch, n_pairs),
        in_specs=specs,
        out_specs=pl.BlockSpec((None, seq, LANES), lambda b, hp: (b, 0, hp)),
        out_shape=jax.ShapeDtypeStruct((batch, seq, width), BF16),
        scratch_shapes=[pltpu.VMEM((n_g, seq, LANES), F32), pltpu.VMEM((n_g, seq, LANES), F32)],
        compiler_params=_params(("parallel", "parallel"), vmem),
        name="dilated_attention",
    )(*args)


def _log_sigmoid(z):
    return jnp.minimum(z, 0.0) - jnp.log1p(jnp.exp(-jnp.abs(z)))


def _mlstm_proj_kernel(x_ref, g_ref, wq_ref, wkt_ref, wv_ref, wo_ref, wgc_ref, wgr_ref, bc_ref, br_ref,
                       q_ref, kt_ref, v_ref, og_ref, gc_ref, gr_ref):
    xn = _rms(x_ref[...], g_ref[...]).astype(BF16)
    q_ref[...] = _dot(xn, wq_ref[...]).astype(BF16)
    kt_ref[...] = (_dot_nt(wkt_ref[...], xn) * (ML_DQK ** -0.5)).astype(BF16)
    v_ref[...] = _dot(xn, wv_ref[...]).astype(BF16)
    og_ref[...] = _dot(xn, wo_ref[...])
    zc = _dot(xn, wgc_ref[...]) + bc_ref[...]
    lane = lax.broadcasted_iota(jnp.int32, zc.shape, 1)
    gc_ref[...] = jnp.where(lane >= ML_HEADS, _log_sigmoid(zc), zc)
    zr = _dot_nt(wgr_ref[...], xn) + br_ref[...]
    rowi = lax.broadcasted_iota(jnp.int32, zr.shape, 0)
    gr_ref[...] = jnp.where(rowi >= ML_HEADS, _log_sigmoid(zr), zr)


def _mlstm_proj(x, g_pre, w_in, b_i, b_f, *, tm=512):
    m, d = x.shape
    nh = ML_HEADS
    o1 = nh * ML_DQK
    o2 = 2 * o1
    o3 = o2 + nh * ML_DV
    o4 = o3 + nh * ML_DV
    wq = w_in[:, :o1].astype(BF16)
    wkt = w_in[:, o1:o2].T.astype(BF16)
    wv = w_in[:, o2:o3].astype(BF16)
    wo = w_in[:, o3:o4].astype(BF16)
    wg = w_in[:, o4:]
    wgc = jnp.pad(wg, ((0, 0), (0, LANES - 2 * nh))).astype(BF16)
    wgr = wg.T.astype(BF16)
    bias = jnp.concatenate([b_i, b_f]).astype(F32)
    bc = jnp.pad(bias, (0, LANES - 2 * nh))[None, :]
    br = bias[:, None]
    tm = min(tm, m)
    assert m % tm == 0 and 2 * nh == SUBLANES
    dv = nh * ML_DV
    vmem = (2 * tm * d * 4 + 2 * d * (2 * o1 + 2 * dv + LANES + SUBLANES) * 2
            + 2 * tm * (2 * o1 * 2 + dv * 2 + dv * 4 + LANES * 4 + SUBLANES * 4)
            + tm * (2 * o1 + 2 * dv) * 4 + tm * d * 6 + (4 << 20))
    full = lambda shape: pl.BlockSpec(shape, lambda i: (0, 0))
    return pl.pallas_call(
        _mlstm_proj_kernel,
        grid=(m // tm,),
        in_specs=[
            pl.BlockSpec((tm, d), lambda i: (i, 0)),
            full((1, d)), full((d, o1)), full((o1, d)), full((d, dv)), full((d, dv)),
            full((d, LANES)), full((2 * nh, d)), full((1, LANES)), full((2 * nh, 1)),
        ],
        out_specs=[
            pl.BlockSpec((tm, o1), lambda i: (i, 0)),
            pl.BlockSpec((o1, tm), lambda i: (0, i)),
            pl.BlockSpec((tm, dv), lambda i: (i, 0)),
            pl.BlockSpec((tm, dv), lambda i: (i, 0)),
            pl.BlockSpec((tm, LANES), lambda i: (i, 0)),
            pl.BlockSpec((2 * nh, tm), lambda i: (0, i)),
        ],
        out_shape=[
            jax.ShapeDtypeStruct((m, o1), BF16),
            jax.ShapeDtypeStruct((o1, m), BF16),
            jax.ShapeDtypeStruct((m, dv), BF16),
            jax.ShapeDtypeStruct((m, dv), F32),
            jax.ShapeDtypeStruct((m, LANES), F32),
            jax.ShapeDtypeStruct((2 * nh, m), F32),
        ],
        compiler_params=_params(("parallel",), vmem),
        name="mlstm_proj",
    )(x, g_pre, wq, wkt, wv, wo, wgc, wgr, bc, br)


def _mlstm_kernel(q_ref, kt_ref, v_ref, og_ref, gc_ref, gr_ref, o_ref, st_ref, *, seq):
    nh, lc, dqk, dv = ML_HEADS, ML_CHUNK, ML_DQK, ML_DV
    st_ref[...] = jnp.zeros(st_ref.shape, F32)
    row = lax.broadcasted_iota(jnp.int32, (lc, lc), 0)
    col = lax.broadcasted_iota(jnp.int32, (lc, lc), 1)
    causal = col <= row
    tril = causal.astype(F32)
    triu = (row <= col).astype(F32)
    lane0 = lax.broadcasted_iota(jnp.int32, (lc, LANES), 1) == 0

    def body(j, ms):
        pos = pl.ds(pl.multiple_of(j * lc, lc), lc)
        gr_c = gr_ref[:, pos]
        gc_c = gc_ref[pos, :]
        b_rows = _dot_f32(gr_c, triu)
        b_cols = _dot_f32(tril, gc_c)
        new_ms = []
        for h in range(nh):
            m_h = ms[h]
            g_row = gr_c[h:h + 1, :] - b_rows[nh + h:nh + h + 1, :]
            b_col = b_cols[:, nh + h:nh + h + 1]
            g_col = gc_c[:, h:h + 1] - b_col
            dmat = jnp.where(causal, g_row, NEG_INF)
            mu = jnp.maximum(m_h, jnp.max(dmat, axis=1, keepdims=True))
            w = jnp.exp(dmat - mu)
            wi = jnp.exp(m_h - mu)
            emt = jnp.exp(-(b_col + mu))
            qh = q_ref[pos, h * dqk:(h + 1) * dqk]
            kth = kt_ref[h * dqk:(h + 1) * dqk, pos]
            vh = v_ref[pos, h * dv:(h + 1) * dv]
            a = _dot(qh, kth) * w
            state = st_ref[h]
            qc = _dot(qh, state.astype(BF16))
            num = _dot(a.astype(BF16), vh) + wi * qc[:, :dv]
            den = jnp.sum(a, axis=1, keepdims=True) + wi * qc[:, dv:dv + 1]
            hval = num / jnp.maximum(jnp.abs(den), emt)
            gate = jax.nn.sigmoid(og_ref[pos, h * dv:(h + 1) * dv])
            o_ref[pos, h * dv:(h + 1) * dv] = (hval * gate).astype(BF16)
            mu_last = mu[lc - 1:lc, :]
            we = jnp.exp(g_col - mu_last)
            decay = jnp.exp(m_h - mu_last)
            vaug = jnp.concatenate([vh.astype(F32) * we, jnp.where(lane0, we, 0.0)], axis=1)
            st_ref[h] = decay * state + _dot(kth, vaug.astype(BF16))
            new_ms.append(b_col[lc - 1:lc, :] + mu_last)
        return tuple(new_ms)

    lax.fori_loop(0, seq // lc, body, tuple(jnp.zeros((1, 1), F32) for _ in range(nh)))


def _mlstm_scan(q, kt, v, og, gc, gr, *, batch, seq):
    nh, dqk, dv = ML_HEADS, ML_DQK, ML_DV
    assert seq % ML_CHUNK == 0
    vmem = (2 * seq * (2 * nh * dqk * 2 + nh * dv * 2 + nh * dv * 4 + LANES * 4 + SUBLANES * 4 + nh * dv * 2)
            + nh * dqk * (dv + LANES) * 4 + (8 << 20))
    return pl.pallas_call(
        functools.partial(_mlstm_kernel, seq=seq),
        grid=(batch,),
        in_specs=[
            pl.BlockSpec((seq, nh * dqk), lambda b: (b, 0)),
            pl.BlockSpec((nh * dqk, seq), lambda b: (0, b)),
            pl.BlockSpec((seq, nh * dv), lambda b: (b, 0)),
            pl.BlockSpec((seq, nh * dv), lambda b: (b, 0)),
            pl.BlockSpec((seq, LANES), lambda b: (b, 0)),
            pl.BlockSpec((2 * nh, seq), lambda b: (0, b)),
        ],
        out_specs=pl.BlockSpec((seq, nh * dv), lambda b: (b, 0)),
        out_shape=jax.ShapeDtypeStruct((batch * seq, nh * dv), BF16),
        scratch_shapes=[pltpu.VMEM((nh, dqk, dv + LANES), F32)],
        compiler_params=_params(("parallel",), vmem),
        name="mlstm_scan",
    )(q, kt, v, og, gc, gr)


def kernel(x, norm_g, ffn1_wg, ffn1_wu, ffn1_wd, ffn2_wg, ffn2_wu, ffn2_wd,
           conv_w_in, conv_k, conv_w_out, attn_w_in, attn_w_out,
           mlstm_w_in, mlstm_b_i, mlstm_b_f, mlstm_w_out):
    batch, seq, d = x.shape
    depth = norm_g.shape[0]
    h = x.reshape(batch * seq, d)
    dils = tuple(dil for _, dil in DILATION_PAIRS)
    width = ATT_HEADS * ATT_HD
    for i in range(depth):
        g = [norm_g[i, k][None, :] for k in range(norm_g.shape[1])]
        h = _ffn(h, g[0], g[1], ffn1_wg[i].astype(BF16), ffn1_wu[i].astype(BF16), ffn1_wd[i].astype(BF16))
        mixer, j = i % N_MIXERS, i // N_MIXERS
        if mixer == 0:
            h = _conv_mixer(h, g[2], g[3], conv_w_in[j].astype(BF16), conv_k[j],
                            conv_w_out[j].astype(BF16), seq=seq)
        elif mixer == 1:
            w_in = attn_w_in[j].astype(BF16)
            qkvs = [_qkv_proj(h, g[2], w_in[:, gi * 3 * width:(gi + 1) * 3 * width],
                              batch=batch, seq=seq, dil=dil) for gi, dil in enumerate(dils)]
            att = _attention(qkvs, batch=batch, seq=seq, dils=dils, width=width)
            h = _outproj(att.reshape(batch * seq, width), h, attn_w_out[j].astype(BF16), g[3])
        else:
            q, kt, v, og, gc, gr = _mlstm_proj(h, g[2], mlstm_w_in[j], mlstm_b_i[j], mlstm_b_f[j])
            hg = _mlstm_scan(q, kt, v, og, gc, gr, batch=batch, seq=seq)
            h = _outproj(hg, h, mlstm_w_out[j].astype(BF16), g[3])
        h = _ffn(h, g[4], g[5], ffn2_wg[i].astype(BF16), ffn2_wu[i].astype(BF16), ffn2_wd[i].astype(BF16))
    return h.reshape(batch, seq, d)
```

```python
import functools
import math

import jax
import jax.numpy as jnp
from jax import lax
from jax.experimental import pallas as pl
from jax.experimental.pallas import tpu as pltpu

LANES = 128
SUBLANES = 8
V7X_VMEM_BYTES = 64 * 1024 * 1024

EPS = 1e-6
CONV_W = 3
ATT_HEADS = 16
ATT_HD = 64
DILATION_PAIRS = ((128, 1), (512, 4), (2048, 16))
ATT_BLOCK = 128
ROPE_THETA = 500000.0
ROT_DIM = ATT_HD // 4
ML_HEADS = 4
ML_DQK = 128
ML_DV = 256
ML_CHUNK = 128
N_MIXERS = 3

BF16 = jnp.bfloat16
F32 = jnp.float32
NEG_INF = float("-inf")


def _params(semantics, vmem_bytes):
    return pltpu.CompilerParams(dimension_semantics=semantics,
                                vmem_limit_bytes=min(int(vmem_bytes), V7X_VMEM_BYTES - (4 << 20)))


def _rms(x32, g_row):
    ms = jnp.mean(x32 * x32, axis=-1, keepdims=True)
    return (x32 * lax.rsqrt(ms + EPS)) * g_row


def _dot(a, b):
    return jnp.dot(a, b, preferred_element_type=F32)


def _dot_nt(a, b):
    return lax.dot_general(a, b, (((1,), (1,)), ((), ())), preferred_element_type=F32)


def _dot_f32(a, b):
    return jnp.dot(a, b, preferred_element_type=F32, precision=lax.Precision.HIGHEST)


def _ffn_kernel(x_ref, gpre_ref, gpost_ref, wg_ref, wu_ref, wd_ref, o_ref, acc_ref, *, cw):
    xn = _rms(x_ref[...], gpre_ref[...]).astype(BF16)
    ff = wg_ref.shape[1]
    for c in range(ff // cw):
        lo = c * cw
        g = _dot(xn, wg_ref[:, lo:lo + cw])
        u = _dot(xn, wu_ref[:, lo:lo + cw])
        h = ((g * jax.nn.sigmoid(g)) * u).astype(BF16)
        part = _dot(h, wd_ref[lo:lo + cw, :])
        if c == 0:
            acc_ref[...] = part
        else:
            acc_ref[...] += part
    o_ref[...] = x_ref[...] + 0.5 * _rms(acc_ref[...], gpost_ref[...])


def _resident(shape):
    return pl.BlockSpec(shape, lambda *_: (0,) * len(shape), pipeline_mode=pl.Buffered(1))


def _ffn(x, g_pre, g_post, wg, wu, wd, *, tm=1024, cw=256):
    m, d = x.shape
    ff = wg.shape[1]
    tm = min(tm, m)
    assert m % tm == 0 and ff % cw == 0
    vmem = (4 * tm * d * 4
            + 3 * d * ff * 2
            + tm * d * (2 + 4 + 4)
            + 2 * tm * cw * (4 + 4 + 2 + 4)
            + (4 << 20))
    return pl.pallas_call(
        functools.partial(_ffn_kernel, cw=cw),
        grid=(m // tm,),
        in_specs=[
            pl.BlockSpec((tm, d), lambda i: (i, 0)),
            _resident((1, d)), _resident((1, d)),
            _resident((d, ff)), _resident((d, ff)), _resident((ff, d)),
        ],
        out_specs=pl.BlockSpec((tm, d), lambda i: (i, 0)),
        out_shape=jax.ShapeDtypeStruct((m, d), F32),
        scratch_shapes=[pltpu.VMEM((tm, d), F32)],
        compiler_params=_params(("parallel",), vmem),
        name="ffn",
    )(x, g_pre, g_post, wg, wu, wd)


def _outproj_kernel(a_ref, x_ref, w_ref, g_ref, o_ref):
    y = _dot(a_ref[...], w_ref[...])
    o_ref[...] = x_ref[...] + _rms(y, g_ref[...])


def _outproj(a, x, w, g_post, *, tm=512):
    m, d = x.shape
    k = a.shape[1]
    tm = min(tm, m)
    assert m % tm == 0
    vmem = 2 * tm * k * 2 + 4 * tm * d * 4 + 2 * k * d * 2 + 2 * tm * d * 4 + (4 << 20)
    return pl.pallas_call(
        _outproj_kernel,
        grid=(m // tm,),
        in_specs=[
            pl.BlockSpec((tm, k), lambda i: (i, 0)),
            pl.BlockSpec((tm, d), lambda i: (i, 0)),
            pl.BlockSpec((k, d), lambda i: (0, 0)),
            pl.BlockSpec((1, d), lambda i: (0, 0)),
        ],
        out_specs=pl.BlockSpec((tm, d), lambda i: (i, 0)),
        out_shape=jax.ShapeDtypeStruct((m, d), F32),
        compiler_params=_params(("parallel",), vmem),
        name="outproj",
    )(a, x, w, g_post)


def _conv_kernel(x_ref, gpre_ref, gpost_ref, win_ref, k_ref, wout_ref, o_ref, ext_ref, *, tiles_per_seq):
    i = pl.program_id(0)
    tm, d = x_ref.shape
    halo = SUBLANES
    xn = _rms(x_ref[...], gpre_ref[...]).astype(BF16)
    p = _dot(xn, win_ref[...])
    cu = p[:, d:2 * d] * p[:, 2 * d:3 * d]

    @pl.when(i % tiles_per_seq == 0)
    def _():
        ext_ref[0:halo, :] = jnp.zeros((halo, d), F32)

    ext_ref[halo:halo + tm, :] = cu
    kk = k_ref[...]
    conv = (kk[0:1, :] * cu
            + kk[1:2, :] * ext_ref[halo - 1:halo - 1 + tm, :]
            + kk[2:3, :] * ext_ref[halo - 2:halo - 2 + tm, :])
    ext_ref[0:halo, :] = cu[tm - halo:tm, :]
    y = _dot((p[:, 0:d] * conv).astype(BF16), wout_ref[...])
    o_ref[...] = x_ref[...] + _rms(y, gpost_ref[...])


def _conv_mixer(x, g_pre, g_post, w_in, k, w_out, *, seq, tm=512):
    m, d = x.shape
    tm = min(tm, seq)
    assert seq % tm == 0 and m % seq == 0
    vmem = (4 * tm * d * 4 + 2 * (3 * d * d + d * d) * 2 + (tm + SUBLANES) * d * 4
            + tm * 3 * d * 4 + 6 * tm * d * 4 + (4 << 20))
    return pl.pallas_call(
        functools.partial(_conv_kernel, tiles_per_seq=seq // tm),
        grid=(m // tm,),
        in_specs=[
            pl.BlockSpec((tm, d), lambda i: (i, 0)),
            pl.BlockSpec((1, d), lambda i: (0, 0)),
            pl.BlockSpec((1, d), lambda i: (0, 0)),
            pl.BlockSpec((d, 3 * d), lambda i: (0, 0)),
            pl.BlockSpec((CONV_W, d), lambda i: (0, 0)),
            pl.BlockSpec((d, d), lambda i: (0, 0)),
        ],
        out_specs=pl.BlockSpec((tm, d), lambda i: (i, 0)),
        out_shape=jax.ShapeDtypeStruct((m, d), F32),
        scratch_shapes=[pltpu.VMEM((tm + SUBLANES, d), F32)],
        compiler_params=_params(("arbitrary",), vmem),
        name="conv_mixer",
    )(x, g_pre, g_post, w_in, k, w_out)


def _rope_tables(seq, dil, rows):
    length = seq // dil
    u = jnp.arange(rows, dtype=jnp.int32) % length
    pos = (u[None, :] * dil + jnp.arange(dil, dtype=jnp.int32)[:, None]).astype(F32)
    inv = ROPE_THETA ** (-jnp.arange(0, ROT_DIM, 2, dtype=F32) / ROT_DIM)
    ang = pos[:, :, None] * inv[None, None, :]
    cos, sin = jnp.cos(ang), jnp.sin(ang)
    half = ROT_DIM // 2
    dd = jnp.arange(LANES) % ATT_HD
    fi = dd % half
    cos_l = jnp.take(cos, fi, axis=-1)
    sin_l = jnp.take(sin, fi, axis=-1)
    c_tab = jnp.where(dd < ROT_DIM, cos_l, 1.0)
    s_lo = jnp.where(dd < half, -sin_l, 0.0)
    s_hi = jnp.where((dd >= half) & (dd < ROT_DIM), sin_l, 0.0)
    return c_tab.astype(F32), s_lo.astype(F32), s_hi.astype(F32)


def _qkv_kernel(x_ref, g_ref, w_ref, c_ref, slo_ref, shi_ref, o_ref, *, width):
    xn = _rms(x_ref[...], g_ref[...]).astype(BF16)
    p = _dot(xn, w_ref[...])
    c_tab, s_lo, s_hi = c_ref[...], slo_ref[...], shi_ref[...]
    half = ROT_DIM // 2
    scale = ATT_HD ** -0.5
    for part in range(2):
        for cb in range(width // LANES):
            lo = part * width + cb * LANES
            blk = p[:, lo:lo + LANES]
            rot = (blk * c_tab
                   + pltpu.roll(blk, LANES - half, axis=1) * s_lo
                   + pltpu.roll(blk, half, axis=1) * s_hi)
            if part == 0:
                rot = rot * scale
            o_ref[:, lo:lo + LANES] = rot.astype(BF16)
    o_ref[:, 2 * width:3 * width] = p[:, 2 * width:3 * width].astype(BF16)


def _qkv_proj(x, g_pre, w, *, group, width, batch, seq, dil, tm=512):
    m, d = x.shape
    length = seq // dil
    rows = m // dil
    tm = min(tm, rows)
    assert rows % tm == 0 and (length % tm == 0 or tm % length == 0)
    tab_rows = max(length, tm)
    c_tab, s_lo, s_hi = _rope_tables(seq, dil, tab_rows)
    tab_blocks = tab_rows // tm
    xv = x.reshape(rows, dil * d)
    tab_spec = pl.BlockSpec((None, tm, LANES), lambda i, r: (r, i % tab_blocks, 0))
    vmem = (2 * tm * d * 4 + 2 * d * 3 * width * 2 + 2 * tm * 3 * width * 2 + 6 * tm * LANES * 4
            + tm * 3 * width * 4 + tm * d * 6 + (4 << 20))
    return pl.pallas_call(
        functools.partial(_qkv_kernel, width=width),
        grid=(rows // tm, dil),
        in_specs=[
            pl.BlockSpec((tm, d), lambda i, r: (i, r)),
            pl.BlockSpec((1, d), lambda i, r: (0, 0)),
            pl.BlockSpec((d, 3 * width), lambda i, r: (0, group)),
            tab_spec, tab_spec, tab_spec,
        ],
        out_specs=pl.BlockSpec((None, tm, 3 * width), lambda i, r: (r, i, 0)),
        out_shape=jax.ShapeDtypeStruct((dil, rows, 3 * width), BF16),
        compiler_params=_params(("parallel", "parallel"), vmem),
        name=f"qkv_proj_d{dil}",
    )(xv, g_pre, w, c_tab, s_lo, s_hi)


def _bdot_nt(a, b):
    return lax.dot_general(a, b, (((2,), (2,)), ((0,), (0,))), preferred_element_type=F32)


def _bdot(a, b):
    return lax.dot_general(a, b, (((2,), (1,)), ((0,), (0,))), preferred_element_type=F32)


def _attn_blocks(q, kc, vc, kp, vp, first_ok):
    shape = q.shape
    gidx = lax.broadcasted_iota(jnp.int32, shape, 0)
    row = lax.broadcasted_iota(jnp.int32, shape, 1)
    lane = lax.broadcasted_iota(jnp.int32, shape, 2)
    keep_cur = lane <= row
    hi_head = lane >= ATT_HD
    zero = jnp.zeros_like(q)
    if kp is not None:
        keep_prev = jnp.logical_and(lane >= row, jnp.logical_or(gidx > 0, first_ok))
    outs, lses = [], []
    for hh in range(2):
        hsel = hi_head if hh == 1 else jnp.logical_not(hi_head)
        qm = jnp.where(hsel, q, zero)
        s_c = jnp.where(keep_cur, _bdot_nt(qm, kc), NEG_INF)
        if kp is not None:
            s_p = jnp.where(keep_prev, _bdot_nt(qm, kp), NEG_INF)
            mx = jnp.max(jnp.maximum(s_c, s_p), axis=2, keepdims=True)
            p_c = jnp.exp(s_c - mx)
            p_p = jnp.exp(s_p - mx)
            den = jnp.sum(p_c + p_p, axis=2, keepdims=True)
            num = _bdot(p_c.astype(BF16), vc) + _bdot(p_p.astype(BF16), vp)
        else:
            mx = jnp.max(s_c, axis=2, keepdims=True)
            p_c = jnp.exp(s_c - mx)
            den = jnp.sum(p_c, axis=2, keepdims=True)
            num = _bdot(p_c.astype(BF16), vc)
        outs.append(num / den)
        lses.append(jnp.broadcast_to(mx + jnp.log(den), shape))
    return jnp.where(hi_head, outs[1], outs[0]), jnp.where(hi_head, lses[1], lses[0])


def _attn_kernel(*refs, dils, seq, gb):
    n_g = len(dils)
    qkv = refs[:3 * n_g]
    o_ref = refs[3 * n_g]
    oacc_ref, lacc_ref = refs[3 * n_g + 1], refs[3 * n_g + 2]
    blk = ATT_BLOCK
    span = gb * blk

    for g, dil in enumerate(dils):
        q_ref, k_ref, v_ref = qkv[3 * g:3 * g + 3]
        nb = seq // dil // blk
        assert nb == 1 or nb % gb == 0

        def body(c, carry, q_ref=q_ref, k_ref=k_ref, v_ref=v_ref, nb=nb, dil=dil, g=g):
            if nb == 1:
                res = pl.ds(c * gb, gb)
                out, lse = _attn_blocks(q_ref[res], k_ref[res], v_ref[res], None, None, None)
                for t in range(gb):
                    dst = pl.ds(c * gb + t, blk, stride=dil)
                    oacc_ref[g, dst, :] = out[t]
                    lacc_ref[g, dst, :] = lse[t]
                return carry
            r = (c * gb) // nb
            n0 = (c * gb) % nb
            cur = pl.ds(pl.multiple_of(n0 * blk, blk), span)
            first = pl.ds(pl.multiple_of(jnp.maximum(n0 - 1, 0) * blk, blk), blk)
            q = q_ref[r, cur, :].reshape(gb, blk, LANES)
            kc = k_ref[r, cur, :].reshape(gb, blk, LANES)
            vc = v_ref[r, cur, :].reshape(gb, blk, LANES)
            kp = jnp.concatenate([k_ref[r, first, :][None], kc[:gb - 1]], axis=0)
            vp = jnp.concatenate([v_ref[r, first, :][None], vc[:gb - 1]], axis=0)
            out, lse = _attn_blocks(q, kc, vc, kp, vp, n0 > 0)
            start = r + dil * n0 * blk
            if dil == 1:
                dst = pl.ds(pl.multiple_of(start, blk), span)
            else:
                dst = pl.ds(start, span, stride=dil)
            oacc_ref[g, dst, :] = out.reshape(span, LANES)
            lacc_ref[g, dst, :] = lse.reshape(span, LANES)
            return carry

        lax.fori_loop(0, seq // span, body, 0)

    rows = 2 * blk

    def mix(t, carry):
        sl = pl.ds(pl.multiple_of(t * rows, rows), rows)
        ls = [lacc_ref[g, sl, :] for g in range(n_g)]
        mx = functools.reduce(jnp.maximum, ls)
        es = [jnp.exp(l - mx) for l in ls]
        inv = 1.0 / functools.reduce(jnp.add, es)
        acc = None
        for g in range(n_g):
            term = (es[g] * inv) * oacc_ref[g, sl, :]
            acc = term if acc is None else acc + term
        o_ref[sl, :] = acc.astype(BF16)
        return carry

    lax.fori_loop(0, seq // rows, mix, 0)


def _attention(qkvs, *, batch, seq, dils, width, gb=4):
    n_pairs = width // LANES
    args, specs = [], []
    for arr, dil in zip(qkvs, dils):
        length = seq // dil
        for part in range(3):
            args.append(arr)
            specs.append(pl.BlockSpec((dil, length, LANES),
                                      lambda b, hp, part=part: (0, b, part * n_pairs + hp)))
    n_g = len(dils)
    vmem = (2 * 3 * n_g * seq * LANES * 2 + 2 * n_g * seq * LANES * 4 + 2 * seq * LANES * 2
            + 24 * gb * ATT_BLOCK * LANES * 4 + (8 << 20))
    return pl.pallas_call(
        functools.partial(_attn_kernel, dils=tuple(dils), seq=seq, gb=gb),
        grid=(batch, n_pairs),
        in_specs=specs,
        out_specs=pl.BlockSpec((None, seq, LANES), lambda b, hp: (b, 0, hp)),
        out_shape=jax.ShapeDtypeStruct((batch, seq, width), BF16),
        scratch_shapes=[pltpu.VMEM((n_g, seq, LANES), F32), pltpu.VMEM((n_g, seq, LANES), F32)],
        compiler_params=_params(("parallel", "parallel"), vmem),
        name="dilated_attention",
    )(*args)


def _log_sigmoid(z):
    return jnp.minimum(z, 0.0) - jnp.log1p(jnp.exp(-jnp.abs(z)))


def _mlstm_proj_kernel(x_ref, g_ref, wq_ref, wkt_ref, wv_ref, wo_ref, wgc_ref, wgr_ref, bc_ref, br_ref,
                       q_ref, kt_ref, v_ref, og_ref, gc_ref, gr_ref):
    xn = _rms(x_ref[...], g_ref[...]).astype(BF16)
    q_ref[...] = _dot(xn, wq_ref[...]).astype(BF16)
    kt_ref[...] = (_dot_nt(wkt_ref[...], xn) * (ML_DQK ** -0.5)).astype(BF16)
    v_ref[...] = _dot(xn, wv_ref[...]).astype(BF16)
    og_ref[...] = _dot(xn, wo_ref[...])
    zc = _dot(xn, wgc_ref[...]) + bc_ref[...]
    lane = lax.broadcasted_iota(jnp.int32, zc.shape, 1)
    gc_ref[...] = jnp.where(lane >= ML_HEADS, _log_sigmoid(zc), zc)
    zr = _dot_nt(wgr_ref[...], xn) + br_ref[...]
    rowi = lax.broadcasted_iota(jnp.int32, zr.shape, 0)
    gr_ref[...] = jnp.where(rowi >= ML_HEADS, _log_sigmoid(zr), zr)


def _mlstm_proj(x, g_pre, w_in, b_i, b_f, *, tm=512):
    m, d = x.shape
    nh = ML_HEADS
    o1 = nh * ML_DQK
    o2 = 2 * o1
    o3 = o2 + nh * ML_DV
    o4 = o3 + nh * ML_DV
    wq = w_in[:, :o1].astype(BF16)
    wkt = w_in[:, o1:o2].T.astype(BF16)
    wv = w_in[:, o2:o3].astype(BF16)
    wo = w_in[:, o3:o4].astype(BF16)
    wg = w_in[:, o4:]
    wgc = jnp.pad(wg, ((0, 0), (0, LANES - 2 * nh))).astype(BF16)
    wgr = wg.T.astype(BF16)
    bias = jnp.concatenate([b_i, b_f]).astype(F32)
    bc = jnp.pad(bias, (0, LANES - 2 * nh))[None, :]
    br = bias[:, None]
    tm = min(tm, m)
    assert m % tm == 0 and 2 * nh == SUBLANES
    dv = nh * ML_DV
    vmem = (2 * tm * d * 4 + 2 * d * (2 * o1 + 2 * dv + LANES + SUBLANES) * 2
            + 2 * tm * (2 * o1 * 2 + dv * 2 + dv * 4 + LANES * 4 + SUBLANES * 4)
            + tm * (2 * o1 + 2 * dv) * 4 + tm * d * 6 + (4 << 20))
    full = lambda shape: pl.BlockSpec(shape, lambda i: (0, 0))
    return pl.pallas_call(
        _mlstm_proj_kernel,
        grid=(m // tm,),
        in_specs=[
            pl.BlockSpec((tm, d), lambda i: (i, 0)),
            full((1, d)), full((d, o1)), full((o1, d)), full((d, dv)), full((d, dv)),
            full((d, LANES)), full((2 * nh, d)), full((1, LANES)), full((2 * nh, 1)),
        ],
        out_specs=[
            pl.BlockSpec((tm, o1), lambda i: (i, 0)),
            pl.BlockSpec((o1, tm), lambda i: (0, i)),
            pl.BlockSpec((tm, dv), lambda i: (i, 0)),
            pl.BlockSpec((tm, dv), lambda i: (i, 0)),
            pl.BlockSpec((tm, LANES), lambda i: (i, 0)),
            pl.BlockSpec((2 * nh, tm), lambda i: (0, i)),
        ],
        out_shape=[
            jax.ShapeDtypeStruct((m, o1), BF16),
            jax.ShapeDtypeStruct((o1, m), BF16),
            jax.ShapeDtypeStruct((m, dv), BF16),
            jax.ShapeDtypeStruct((m, dv), F32),
            jax.ShapeDtypeStruct((m, LANES), F32),
            jax.ShapeDtypeStruct((2 * nh, m), F32),
        ],
        compiler_params=_params(("parallel",), vmem),
        name="mlstm_proj",
    )(x, g_pre, wq, wkt, wv, wo, wgc, wgr, bc, br)


def _mlstm_kernel(q_ref, kt_ref, v_ref, og_ref, gc_ref, gr_ref, o_ref, st_ref, *, seq):
    nh, lc, dqk, dv = ML_HEADS, ML_CHUNK, ML_DQK, ML_DV
    st_ref[...] = jnp.zeros(st_ref.shape, F32)
    row = lax.broadcasted_iota(jnp.int32, (lc, lc), 0)
    col = lax.broadcasted_iota(jnp.int32, (lc, lc), 1)
    causal = col <= row
    tril = causal.astype(F32)
    triu = (row <= col).astype(F32)
    lane0 = lax.broadcasted_iota(jnp.int32, (lc, LANES), 1) == 0

    def body(j, ms):
        pos = pl.ds(pl.multiple_of(j * lc, lc), lc)
        gr_c = gr_ref[:, pos]
        gc_c = gc_ref[pos, :]
        b_rows = _dot_f32(gr_c, triu)
        b_cols = _dot_f32(tril, gc_c)
        new_ms = []
        for h in range(nh):
            m_h = ms[h]
            g_row = gr_c[h:h + 1, :] - b_rows[nh + h:nh + h + 1, :]
            b_col = b_cols[:, nh + h:nh + h + 1]
            g_col = gc_c[:, h:h + 1] - b_col
            dmat = jnp.where(causal, g_row, NEG_INF)
            mu = jnp.maximum(m_h, jnp.max(dmat, axis=1, keepdims=True))
            w = jnp.exp(dmat - mu)
            wi = jnp.exp(m_h - mu)
            emt = jnp.exp(-(b_col + mu))
            qh = q_ref[pos, h * dqk:(h + 1) * dqk]
            kth = kt_ref[h * dqk:(h + 1) * dqk, pos]
            vh = v_ref[pos, h * dv:(h + 1) * dv]
            a = _dot(qh, kth) * w
            state = st_ref[h]
            qc = _dot(qh, state.astype(BF16))
            num = _dot(a.astype(BF16), vh) + wi * qc[:, :dv]
            den = jnp.sum(a, axis=1, keepdims=True) + wi * qc[:, dv:dv + 1]
            hval = num / jnp.maximum(jnp.abs(den), emt)
            gate = jax.nn.sigmoid(og_ref[pos, h * dv:(h + 1) * dv])
            o_ref[pos, h * dv:(h + 1) * dv] = (hval * gate).astype(BF16)
            mu_last = mu[lc - 1:lc, :]
            we = jnp.exp(g_col - mu_last)
            decay = jnp.exp(m_h - mu_last)
            vaug = jnp.concatenate([vh.astype(F32) * we, jnp.where(lane0, we, 0.0)], axis=1)
            st_ref[h] = decay * state + _dot(kth, vaug.astype(BF16))
            new_ms.append(b_col[lc - 1:lc, :] + mu_last)
        return tuple(new_ms)

    lax.fori_loop(0, seq // lc, body, tuple(jnp.zeros((1, 1), F32) for _ in range(nh)))


def _mlstm_scan(q, kt, v, og, gc, gr, *, batch, seq):
    nh, dqk, dv = ML_HEADS, ML_DQK, ML_DV
    assert seq % ML_CHUNK == 0
    vmem = (2 * seq * (2 * nh * dqk * 2 + nh * dv * 2 + nh * dv * 4 + LANES * 4 + SUBLANES * 4 + nh * dv * 2)
            + nh * dqk * (dv + LANES) * 4 + (8 << 20))
    return pl.pallas_call(
        functools.partial(_mlstm_kernel, seq=seq),
        grid=(batch,),
        in_specs=[
            pl.BlockSpec((seq, nh * dqk), lambda b: (b, 0)),
            pl.BlockSpec((nh * dqk, seq), lambda b: (0, b)),
            pl.BlockSpec((seq, nh * dv), lambda b: (b, 0)),
            pl.BlockSpec((seq, nh * dv), lambda b: (b, 0)),
            pl.BlockSpec((seq, LANES), lambda b: (b, 0)),
            pl.BlockSpec((2 * nh, seq), lambda b: (0, b)),
        ],
        out_specs=pl.BlockSpec((seq, nh * dv), lambda b: (b, 0)),
        out_shape=jax.ShapeDtypeStruct((batch * seq, nh * dv), BF16),
        scratch_shapes=[pltpu.VMEM((nh, dqk, dv + LANES), F32)],
        compiler_params=_params(("parallel",), vmem),
        name="mlstm_scan",
    )(q, kt, v, og, gc, gr)


def kernel(x, norm_g, ffn1_wg, ffn1_wu, ffn1_wd, ffn2_wg, ffn2_wu, ffn2_wd,
           conv_w_in, conv_k, conv_w_out, attn_w_in, attn_w_out,
           mlstm_w_in, mlstm_b_i, mlstm_b_f, mlstm_w_out):
    batch, seq, d = x.shape
    depth = norm_g.shape[0]
    h = x.reshape(batch * seq, d)
    dils = tuple(dil for _, dil in DILATION_PAIRS)
    width = ATT_HEADS * ATT_HD
    for i in range(depth):
        g = [norm_g[i, k][None, :] for k in range(norm_g.shape[1])]
        h = _ffn(h, g[0], g[1], ffn1_wg[i].astype(BF16), ffn1_wu[i].astype(BF16), ffn1_wd[i].astype(BF16))
        mixer, j = i % N_MIXERS, i // N_MIXERS
        if mixer == 0:
            h = _conv_mixer(h, g[2], g[3], conv_w_in[j].astype(BF16), conv_k[j],
                            conv_w_out[j].astype(BF16), seq=seq)
        elif mixer == 1:
            w_in = attn_w_in[j].astype(BF16)
            qkvs = [_qkv_proj(h, g[2], w_in, group=gi, width=width, batch=batch, seq=seq, dil=dil)
                    for gi, dil in enumerate(dils)]
            att = _attention(qkvs, batch=batch, seq=seq, dils=dils, width=width)
            h = _outproj(att.reshape(batch * seq, width), h, attn_w_out[j].astype(BF16), g[3])
        else:
            q, kt, v, og, gc, gr = _mlstm_proj(h, g[2], mlstm_w_in[j], mlstm_b_i[j], mlstm_b_f[j])
            hg = _mlstm_scan(q, kt, v, og, gc, gr, batch=batch, seq=seq)
            h = _outproj(hg, h, mlstm_w_out[j].astype(BF16), g[3])
        h = _ffn(h, g[4], g[5], ffn2_wg[i].astype(BF16), ffn2_wu[i].astype(BF16), ffn2_wd[i].astype(BF16))
    return h.reshape(batch, seq, d)
```

```python
import functools
import math

import jax
import jax.numpy as jnp
from jax import lax
from jax.experimental import pallas as pl
from jax.experimental.pallas import tpu as pltpu

LANES = 128
SUBLANES = 8
V7X_VMEM_BYTES = 64 * 1024 * 1024

EPS = 1e-6
CONV_W = 3
ATT_HEADS = 16
ATT_HD = 64
DILATION_PAIRS = ((128, 1), (512, 4), (2048, 16))
ATT_BLOCK = 128
ROPE_THETA = 500000.0
ROT_DIM = ATT_HD // 4
ML_HEADS = 4
ML_DQK = 128
ML_DV = 256
ML_CHUNK = 128
N_MIXERS = 3

BF16 = jnp.bfloat16
F32 = jnp.float32
NEG_INF = float("-inf")


def _params(semantics, vmem_bytes):
    return pltpu.CompilerParams(dimension_semantics=semantics,
                                vmem_limit_bytes=min(int(vmem_bytes), V7X_VMEM_BYTES - (4 << 20)))


def _rms(x32, g_row):
    ms = jnp.mean(x32 * x32, axis=-1, keepdims=True)
    return (x32 * lax.rsqrt(ms + EPS)) * g_row


def _dot(a, b):
    return jnp.dot(a, b, preferred_element_type=F32)


def _dot_nt(a, b):
    return lax.dot_general(a, b, (((1,), (1,)), ((), ())), preferred_element_type=F32)


def _dot_f32(a, b):
    return jnp.dot(a, b, preferred_element_type=F32, precision=lax.Precision.HIGHEST)


def _ffn_kernel(x_ref, gpre_ref, gpost_ref, wg_ref, wu_ref, wd_ref, o_ref, acc_ref, *, cw):
    xn = _rms(x_ref[...], gpre_ref[...]).astype(BF16)
    ff = wg_ref.shape[1]
    for c in range(ff // cw):
        lo = c * cw
        g = _dot(xn, wg_ref[:, lo:lo + cw])
        u = _dot(xn, wu_ref[:, lo:lo + cw])
        h = ((g * jax.nn.sigmoid(g)) * u).astype(BF16)
        part = _dot(h, wd_ref[lo:lo + cw, :])
        if c == 0:
            acc_ref[...] = part
        else:
            acc_ref[...] += part
    o_ref[...] = x_ref[...] + 0.5 * _rms(acc_ref[...], gpost_ref[...])


def _resident(shape):
    return pl.BlockSpec(shape, lambda *_: (0,) * len(shape), pipeline_mode=pl.Buffered(1))


def _ffn(x, g_pre, g_post, wg, wu, wd, *, tm=1024, cw=256):
    m, d = x.shape
    ff = wg.shape[1]
    tm = min(tm, m)
    assert m % tm == 0 and ff % cw == 0
    vmem = (4 * tm * d * 4
            + 3 * d * ff * 2
            + tm * d * (2 + 4 + 4)
            + 2 * tm * cw * (4 + 4 + 2 + 4)
            + (4 << 20))
    return pl.pallas_call(
        functools.partial(_ffn_kernel, cw=cw),
        grid=(m // tm,),
        in_specs=[
            pl.BlockSpec((tm, d), lambda i: (i, 0)),
            _resident((1, d)), _resident((1, d)),
            _resident((d, ff)), _resident((d, ff)), _resident((ff, d)),
        ],
        out_specs=pl.BlockSpec((tm, d), lambda i: (i, 0)),
        out_shape=jax.ShapeDtypeStruct((m, d), F32),
        scratch_shapes=[pltpu.VMEM((tm, d), F32)],
        compiler_params=_params(("parallel",), vmem),
        name="ffn",
    )(x, g_pre, g_post, wg, wu, wd)


def _outproj_kernel(a_ref, x_ref, w_ref, g_ref, o_ref):
    y = _dot(a_ref[...], w_ref[...])
    o_ref[...] = x_ref[...] + _rms(y, g_ref[...])


def _outproj(a, x, w, g_post, *, tm=512):
    m, d = x.shape
    k = a.shape[1]
    tm = min(tm, m)
    assert m % tm == 0
    vmem = 2 * tm * k * 2 + 4 * tm * d * 4 + 2 * k * d * 2 + 2 * tm * d * 4 + (4 << 20)
    return pl.pallas_call(
        _outproj_kernel,
        grid=(m // tm,),
        in_specs=[
            pl.BlockSpec((tm, k), lambda i: (i, 0)),
            pl.BlockSpec((tm, d), lambda i: (i, 0)),
            pl.BlockSpec((k, d), lambda i: (0, 0)),
            pl.BlockSpec((1, d), lambda i: (0, 0)),
        ],
        out_specs=pl.BlockSpec((tm, d), lambda i: (i, 0)),
        out_shape=jax.ShapeDtypeStruct((m, d), F32),
        compiler_params=_params(("parallel",), vmem),
        name="outproj",
    )(a, x, w, g_post)


def _conv_kernel(x_ref, gpre_ref, gpost_ref, win_ref, k_ref, wout_ref, o_ref, ext_ref, *, tiles_per_seq):
    i = pl.program_id(0)
    tm, d = x_ref.shape
    halo = SUBLANES
    xn = _rms(x_ref[...], gpre_ref[...]).astype(BF16)
    p = _dot(xn, win_ref[...])
    cu = p[:, d:2 * d] * p[:, 2 * d:3 * d]

    @pl.when(i % tiles_per_seq == 0)
    def _():
        ext_ref[0:halo, :] = jnp.zeros((halo, d), F32)

    ext_ref[halo:halo + tm, :] = cu
    kk = k_ref[...]
    conv = (kk[0:1, :] * cu
            + kk[1:2, :] * ext_ref[halo - 1:halo - 1 + tm, :]
            + kk[2:3, :] * ext_ref[halo - 2:halo - 2 + tm, :])
    ext_ref[0:halo, :] = cu[tm - halo:tm, :]
    y = _dot((p[:, 0:d] * conv).astype(BF16), wout_ref[...])
    o_ref[...] = x_ref[...] + _rms(y, gpost_ref[...])


def _conv_mixer(x, g_pre, g_post, w_in, k, w_out, *, seq, tm=512):
    m, d = x.shape
    tm = min(tm, seq)
    assert seq % tm == 0 and m % seq == 0
    vmem = (4 * tm * d * 4 + 2 * (3 * d * d + d * d) * 2 + (tm + SUBLANES) * d * 4
            + tm * 3 * d * 4 + 6 * tm * d * 4 + (4 << 20))
    return pl.pallas_call(
        functools.partial(_conv_kernel, tiles_per_seq=seq // tm),
        grid=(m // tm,),
        in_specs=[
            pl.BlockSpec((tm, d), lambda i: (i, 0)),
            pl.BlockSpec((1, d), lambda i: (0, 0)),
            pl.BlockSpec((1, d), lambda i: (0, 0)),
            pl.BlockSpec((d, 3 * d), lambda i: (0, 0)),
            pl.BlockSpec((CONV_W, d), lambda i: (0, 0)),
            pl.BlockSpec((d, d), lambda i: (0, 0)),
        ],
        out_specs=pl.BlockSpec((tm, d), lambda i: (i, 0)),
        out_shape=jax.ShapeDtypeStruct((m, d), F32),
        scratch_shapes=[pltpu.VMEM((tm + SUBLANES, d), F32)],
        compiler_params=_params(("arbitrary",), vmem),
        name="conv_mixer",
    )(x, g_pre, g_post, w_in, k, w_out)


def _rope_tables(seq, dil, rows):
    length = seq // dil
    u = jnp.arange(rows, dtype=jnp.int32) % length
    pos = (u[None, :] * dil + jnp.arange(dil, dtype=jnp.int32)[:, None]).astype(F32)
    inv = ROPE_THETA ** (-jnp.arange(0, ROT_DIM, 2, dtype=F32) / ROT_DIM)
    ang = pos[:, :, None] * inv[None, None, :]
    cos, sin = jnp.cos(ang), jnp.sin(ang)
    half = ROT_DIM // 2
    dd = jnp.arange(LANES) % ATT_HD
    fi = dd % half
    cos_l = jnp.take(cos, fi, axis=-1)
    sin_l = jnp.take(sin, fi, axis=-1)
    c_tab = jnp.where(dd < ROT_DIM, cos_l, 1.0)
    s_lo = jnp.where(dd < half, -sin_l, 0.0)
    s_hi = jnp.where((dd >= half) & (dd < ROT_DIM), sin_l, 0.0)
    return c_tab.astype(F32), s_lo.astype(F32), s_hi.astype(F32)


def _qkv_kernel(x_ref, g_ref, w_ref, c_ref, slo_ref, shi_ref, o_ref, slab_ref, xp_ref, *, width, dil):
    tm, d = x_ref.shape
    per = tm // dil
    xn = _rms(x_ref[...], g_ref[...])
    if dil == 1:
        xp = xn.astype(BF16)
    else:
        for cb in range(d // LANES):
            slab_ref[cb] = xn[:, cb * LANES:(cb + 1) * LANES]
        for r in range(dil):
            for cb in range(d // LANES):
                piece = slab_ref[cb, pl.ds(r, per, stride=dil), :]
                xp_ref[r * per:(r + 1) * per, cb * LANES:(cb + 1) * LANES] = piece.astype(BF16)
        xp = xp_ref[...]
    p = _dot(xp, w_ref[...])
    c_tab = c_ref[...].reshape(tm, LANES)
    s_lo = slo_ref[...].reshape(tm, LANES)
    s_hi = shi_ref[...].reshape(tm, LANES)
    half = ROT_DIM // 2
    scale = ATT_HD ** -0.5
    for part in range(2):
        for cb in range(width // LANES):
            lo = part * width + cb * LANES
            blk = p[:, lo:lo + LANES]
            rot = (blk * c_tab
                   + pltpu.roll(blk, LANES - half, axis=1) * s_lo
                   + pltpu.roll(blk, half, axis=1) * s_hi)
            if part == 0:
                rot = rot * scale
            o_ref[:, :, lo:lo + LANES] = rot.astype(BF16).reshape(dil, per, LANES)
    o_ref[:, :, 2 * width:3 * width] = p[:, 2 * width:3 * width].astype(BF16).reshape(dil, per, width)


def _qkv_proj(x, g_pre, w, *, group, width, batch, seq, dil, tm=512):
    m, d = x.shape
    tm = min(tm, seq)
    per = tm // dil
    assert seq % tm == 0 and tm % dil == 0 and per % (2 * SUBLANES) == 0
    c_tab, s_lo, s_hi = _rope_tables(seq, dil, seq // dil)
    tiles_per_seq = seq // tm
    tab_spec = pl.BlockSpec((dil, per, LANES), lambda i: (0, i % tiles_per_seq, 0))
    vmem = (2 * tm * d * 4 + d * 3 * width * 2 + 2 * tm * 3 * width * 2 + 6 * tm * LANES * 4
            + tm * 3 * width * 4 + tm * d * (4 + 4 + 2) + 4 * tm * width * 4 + (4 << 20))
    return pl.pallas_call(
        functools.partial(_qkv_kernel, width=width, dil=dil),
        grid=(m // tm,),
        in_specs=[
            pl.BlockSpec((tm, d), lambda i: (i, 0)),
            _resident((1, d)),
            pl.BlockSpec((d, 3 * width), lambda i: (0, group), pipeline_mode=pl.Buffered(1)),
            tab_spec, tab_spec, tab_spec,
        ],
        out_specs=pl.BlockSpec((dil, per, 3 * width), lambda i: (0, i, 0)),
        out_shape=jax.ShapeDtypeStruct((dil, m // dil, 3 * width), BF16),
        scratch_shapes=[pltpu.VMEM((d // LANES, tm, LANES), F32), pltpu.VMEM((tm, d), BF16)],
        compiler_params=_params(("parallel",), vmem),
        name=f"qkv_proj_d{dil}",
    )(x, g_pre, w, c_tab, s_lo, s_hi)


def _bdot_nt(a, b):
    return lax.dot_general(a, b, (((2,), (2,)), ((0,), (0,))), preferred_element_type=F32)


def _bdot(a, b):
    return lax.dot_general(a, b, (((2,), (1,)), ((0,), (0,))), preferred_element_type=F32)


def _value_operands(v):
    hi = lax.broadcasted_iota(jnp.int32, v.shape, 2).astype(F32).astype(v.dtype) >= ATT_HD
    one = jnp.ones_like(v)
    return jnp.where(hi, one, v), jnp.where(hi, v, one)


def _attn_blocks(q, kc, voc, kp, vop, first_ok):
    g, blk, _ = q.shape
    shape2 = (g, 2 * blk, LANES)
    row2 = lax.broadcasted_iota(jnp.int32, shape2, 1)
    lane = lax.broadcasted_iota(jnp.int32, shape2, 2)
    row = jnp.bitwise_and(row2, blk - 1)
    q2 = jnp.concatenate([q, q], axis=1)
    qs = jnp.where((row2 >= blk) == (lane >= ATT_HD), q2, jnp.zeros_like(q2))
    s_c = jnp.where(lane <= row, _bdot_nt(qs, kc), NEG_INF)
    if kp is not None:
        gidx = lax.broadcasted_iota(jnp.int32, shape2, 0)
        keep_prev = jnp.logical_and(lane >= row, jnp.logical_or(gidx > 0, first_ok))
        s_p = jnp.where(keep_prev, _bdot_nt(qs, kp), NEG_INF)
        mx = jnp.max(jnp.maximum(s_c, s_p), axis=2, keepdims=True)
        p_p = jnp.exp(s_p - mx).astype(BF16)
    else:
        mx = jnp.max(s_c, axis=2, keepdims=True)
    p_c = jnp.exp(s_c - mx).astype(BF16)
    res = []
    for hh in range(2):
        rows = slice(hh * blk, (hh + 1) * blk)
        if kp is not None:
            res.append(_bdot(jnp.concatenate([p_c[:, rows], p_p[:, rows]], axis=2),
                             jnp.concatenate([voc[hh], vop[hh]], axis=1)))
        else:
            res.append(_bdot(p_c[:, rows], voc[hh]))
    hi = lax.broadcasted_iota(jnp.int32, q.shape, 2) >= ATT_HD
    return (jnp.where(hi, res[1], res[0]), jnp.where(hi, res[0], res[1]),
            jnp.where(hi, mx[:, blk:], mx[:, :blk]))


def _attn_kernel(*refs, dils, seq, gb):
    n_g = len(dils)
    qkv = refs[:3 * n_g]
    o_ref = refs[3 * n_g]
    num_ref, den_ref, max_ref = refs[3 * n_g + 1:3 * n_g + 4]
    blk = ATT_BLOCK
    span = gb * blk

    for g, dil in enumerate(dils):
        q_ref, k_ref, v_ref = qkv[3 * g:3 * g + 3]
        nb = seq // dil // blk
        assert nb == 1 or nb % gb == 0

        def body(c, carry, q_ref=q_ref, k_ref=k_ref, v_ref=v_ref, nb=nb, dil=dil, g=g):
            if nb == 1:
                res = pl.ds(c * gb, gb)
                num, den, mx = _attn_blocks(q_ref[res], k_ref[res], _value_operands(v_ref[res]), None, None, None)
                for t in range(gb):
                    dst = pl.ds(c * gb + t, blk, stride=dil)
                    num_ref[g, dst, :] = num[t]
                    den_ref[g, dst, :] = den[t]
                    max_ref[g, dst, :] = mx[t]
                return carry
            r = (c * gb) // nb
            n0 = (c * gb) % nb
            cur = pl.ds(pl.multiple_of(n0 * blk, blk), span)
            first = pl.ds(pl.multiple_of(jnp.maximum(n0 - 1, 0) * blk, blk), blk)
            q = q_ref[r, cur, :].reshape(gb, blk, LANES)
            kc = k_ref[r, cur, :].reshape(gb, blk, LANES)
            voc = _value_operands(v_ref[r, cur, :].reshape(gb, blk, LANES))
            vof = _value_operands(v_ref[r, first, :][None])
            kp = jnp.concatenate([k_ref[r, first, :][None], kc[:gb - 1]], axis=0)
            vop = [jnp.concatenate([vof[hh], voc[hh][:gb - 1]], axis=0) for hh in range(2)]
            num, den, mx = _attn_blocks(q, kc, voc, kp, vop, n0 > 0)
            start = r + dil * n0 * blk
            if dil == 1:
                dst = pl.ds(pl.multiple_of(start, blk), span)
            else:
                dst = pl.ds(start, span, stride=dil)
            num_ref[g, dst, :] = num.reshape(span, LANES)
            den_ref[g, dst, :] = den.reshape(span, LANES)
            max_ref[g, dst, :] = mx.reshape(span, LANES)
            return carry

        lax.fori_loop(0, seq // span, body, 0, unroll=True)

    rows = 2 * blk

    def mix(t, carry):
        sl = pl.ds(pl.multiple_of(t * rows, rows), rows)
        ms = [max_ref[g, sl, :] for g in range(n_g)]
        top = functools.reduce(jnp.maximum, ms)
        ws = [jnp.exp(m - top) for m in ms]
        total = functools.reduce(jnp.add, [ws[g] * pltpu.roll(den_ref[g, sl, :], ATT_HD, axis=1)
                                           for g in range(n_g)])
        inv = 1.0 / total
        acc = functools.reduce(jnp.add, [(ws[g] * inv) * num_ref[g, sl, :] for g in range(n_g)])
        o_ref[sl, :] = acc.astype(BF16)
        return carry

    lax.fori_loop(0, seq // rows, mix, 0)


def _attention(qkvs, *, batch, seq, dils, width, gb=4):
    n_pairs = width // LANES
    args, specs = [], []
    for arr, dil in zip(qkvs, dils):
        length = seq // dil
        for part in range(3):
            args.append(arr)
            specs.append(pl.BlockSpec((dil, length, LANES),
                                      lambda b, hp, part=part: (0, b, part * n_pairs + hp)))
    n_g = len(dils)
    vmem = (2 * 3 * n_g * seq * LANES * 2 + 3 * n_g * seq * LANES * 4 + 2 * seq * LANES * 2
            + 32 * gb * ATT_BLOCK * LANES * 4 + (8 << 20))
    return pl.pallas_call(
        functools.partial(_attn_kernel, dils=tuple(dils), seq=seq, gb=gb),
        grid=(batch, n_pairs),
        in_specs=specs,
        out_specs=pl.BlockSpec((None, seq, LANES), lambda b, hp: (b, 0, hp)),
        out_shape=jax.ShapeDtypeStruct((batch, seq, width), BF16),
        scratch_shapes=[pltpu.VMEM((n_g, seq, LANES), F32)] * 3,
        compiler_params=_params(("parallel", "parallel"), vmem),
        name="dilated_attention",
    )(*args)


def _log_sigmoid(z):
    return jnp.minimum(z, 0.0) - jnp.log1p(jnp.exp(-jnp.abs(z)))


def _mlstm_proj_kernel(x_ref, g_ref, wq_ref, wkt_ref, wv_ref, wo_ref, wgc_ref, wgr_ref, bc_ref, br_ref,
                       q_ref, kt_ref, v_ref, og_ref, gc_ref, gr_ref):
    xn = _rms(x_ref[...], g_ref[...]).astype(BF16)
    q_ref[...] = _dot(xn, wq_ref[...]).astype(BF16)
    kt_ref[...] = (_dot_nt(wkt_ref[...], xn) * (ML_DQK ** -0.5)).astype(BF16)
    v_ref[...] = _dot(xn, wv_ref[...]).astype(BF16)
    og_ref[...] = _dot(xn, wo_ref[...])
    zc = _dot(xn, wgc_ref[...]) + bc_ref[...]
    lane = lax.broadcasted_iota(jnp.int32, zc.shape, 1)
    gc_ref[...] = jnp.where(lane >= ML_HEADS, _log_sigmoid(zc), zc)
    zr = _dot_nt(wgr_ref[...], xn) + br_ref[...]
    rowi = lax.broadcasted_iota(jnp.int32, zr.shape, 0)
    gr_ref[...] = jnp.where(rowi >= ML_HEADS, _log_sigmoid(zr), zr)


def _mlstm_proj(x, g_pre, w_in, b_i, b_f, *, tm=512):
    m, d = x.shape
    nh = ML_HEADS
    o1 = nh * ML_DQK
    o2 = 2 * o1
    o3 = o2 + nh * ML_DV
    o4 = o3 + nh * ML_DV
    wq = w_in[:, :o1].astype(BF16)
    wkt = w_in[:, o1:o2].T.astype(BF16)
    wv = w_in[:, o2:o3].astype(BF16)
    wo = w_in[:, o3:o4].astype(BF16)
    wg = w_in[:, o4:]
    wgc = jnp.pad(wg, ((0, 0), (0, LANES - 2 * nh))).astype(BF16)
    wgr = wg.T.astype(BF16)
    bias = jnp.concatenate([b_i, b_f]).astype(F32)
    bc = jnp.pad(bias, (0, LANES - 2 * nh))[None, :]
    br = bias[:, None]
    tm = min(tm, m)
    assert m % tm == 0 and 2 * nh == SUBLANES
    dv = nh * ML_DV
    vmem = (2 * tm * d * 4 + 2 * d * (2 * o1 + 2 * dv + LANES + SUBLANES) * 2
            + 2 * tm * (2 * o1 * 2 + dv * 2 + dv * 4 + LANES * 4 + SUBLANES * 4)
            + tm * (2 * o1 + 2 * dv) * 4 + tm * d * 6 + (4 << 20))
    full = lambda shape: pl.BlockSpec(shape, lambda i: (0, 0))
    return pl.pallas_call(
        _mlstm_proj_kernel,
        grid=(m // tm,),
        in_specs=[
            pl.BlockSpec((tm, d), lambda i: (i, 0)),
            full((1, d)), full((d, o1)), full((o1, d)), full((d, dv)), full((d, dv)),
            full((d, LANES)), full((2 * nh, d)), full((1, LANES)), full((2 * nh, 1)),
        ],
        out_specs=[
            pl.BlockSpec((tm, o1), lambda i: (i, 0)),
            pl.BlockSpec((o1, tm), lambda i: (0, i)),
            pl.BlockSpec((tm, dv), lambda i: (i, 0)),
            pl.BlockSpec((tm, dv), lambda i: (i, 0)),
            pl.BlockSpec((tm, LANES), lambda i: (i, 0)),
            pl.BlockSpec((2 * nh, tm), lambda i: (0, i)),
        ],
        out_shape=[
            jax.ShapeDtypeStruct((m, o1), BF16),
            jax.ShapeDtypeStruct((o1, m), BF16),
            jax.ShapeDtypeStruct((m, dv), BF16),
            jax.ShapeDtypeStruct((m, dv), F32),
            jax.ShapeDtypeStruct((m, LANES), F32),
            jax.ShapeDtypeStruct((2 * nh, m), F32),
        ],
        compiler_params=_params(("parallel",), vmem),
        name="mlstm_proj",
    )(x, g_pre, wq, wkt, wv, wo, wgc, wgr, bc, br)


def _mlstm_kernel(q_ref, kt_ref, v_ref, og_ref, gc_ref, gr_ref, o_ref, st_ref, *, seq):
    nh, lc, dqk, dv = ML_HEADS, ML_CHUNK, ML_DQK, ML_DV
    st_ref[...] = jnp.zeros(st_ref.shape, F32)
    row = lax.broadcasted_iota(jnp.int32, (lc, lc), 0)
    col = lax.broadcasted_iota(jnp.int32, (lc, lc), 1)
    causal = col <= row
    tril = causal.astype(F32)
    triu = (row <= col).astype(F32)
    lane0 = lax.broadcasted_iota(jnp.int32, (lc, LANES), 1) == 0

    def body(j, ms):
        pos = pl.ds(pl.multiple_of(j * lc, lc), lc)
        gr_c = gr_ref[:, pos]
        gc_c = gc_ref[pos, :]
        b_rows = _dot_f32(gr_c, triu)
        b_cols = _dot_f32(tril, gc_c)
        new_ms = []
        for h in range(nh):
            m_h = ms[h]
            g_row = gr_c[h:h + 1, :] - b_rows[nh + h:nh + h + 1, :]
            b_col = b_cols[:, nh + h:nh + h + 1]
            g_col = gc_c[:, h:h + 1] - b_col
            dmat = jnp.where(causal, g_row, NEG_INF)
            mu = jnp.maximum(m_h, jnp.max(dmat, axis=1, keepdims=True))
            w = jnp.exp(dmat - mu)
            wi = jnp.exp(m_h - mu)
            emt = jnp.exp(-(b_col + mu))
            qh = q_ref[pos, h * dqk:(h + 1) * dqk]
            kth = kt_ref[h * dqk:(h + 1) * dqk, pos]
            vh = v_ref[pos, h * dv:(h + 1) * dv]
            a = _dot(qh, kth) * w
            state = st_ref[h]
            qc = _dot(qh, state.astype(BF16))
            num = _dot(a.astype(BF16), vh) + wi * qc[:, :dv]
            den = jnp.sum(a, axis=1, keepdims=True) + wi * qc[:, dv:dv + 1]
            hval = num / jnp.maximum(jnp.abs(den), emt)
            gate = jax.nn.sigmoid(og_ref[pos, h * dv:(h + 1) * dv])
            o_ref[pos, h * dv:(h + 1) * dv] = (hval * gate).astype(BF16)
            mu_last = mu[lc - 1:lc, :]
            we = jnp.exp(g_col - mu_last)
            decay = jnp.exp(m_h - mu_last)
            vaug = jnp.concatenate([vh.astype(F32) * we, jnp.where(lane0, we, 0.0)], axis=1)
            st_ref[h] = decay * state + _dot(kth, vaug.astype(BF16))
            new_ms.append(b_col[lc - 1:lc, :] + mu_last)
        return tuple(new_ms)

    lax.fori_loop(0, seq // lc, body, tuple(jnp.zeros((1, 1), F32) for _ in range(nh)))


def _mlstm_scan(q, kt, v, og, gc, gr, *, batch, seq):
    nh, dqk, dv = ML_HEADS, ML_DQK, ML_DV
    assert seq % ML_CHUNK == 0
    vmem = (2 * seq * (2 * nh * dqk * 2 + nh * dv * 2 + nh * dv * 4 + LANES * 4 + SUBLANES * 4 + nh * dv * 2)
            + nh * dqk * (dv + LANES) * 4 + (8 << 20))
    return pl.pallas_call(
        functools.partial(_mlstm_kernel, seq=seq),
        grid=(batch,),
        in_specs=[
            pl.BlockSpec((seq, nh * dqk), lambda b: (b, 0)),
            pl.BlockSpec((nh * dqk, seq), lambda b: (0, b)),
            pl.BlockSpec((seq, nh * dv), lambda b: (b, 0)),
            pl.BlockSpec((seq, nh * dv), lambda b: (b, 0)),
            pl.BlockSpec((seq, LANES), lambda b: (b, 0)),
            pl.BlockSpec((2 * nh, seq), lambda b: (0, b)),
        ],
        out_specs=pl.BlockSpec((seq, nh * dv), lambda b: (b, 0)),
        out_shape=jax.ShapeDtypeStruct((batch * seq, nh * dv), BF16),
        scratch_shapes=[pltpu.VMEM((nh, dqk, dv + LANES), F32)],
        compiler_params=_params(("parallel",), vmem),
        name="mlstm_scan",
    )(q, kt, v, og, gc, gr)


def kernel(x, norm_g, ffn1_wg, ffn1_wu, ffn1_wd, ffn2_wg, ffn2_wu, ffn2_wd,
           conv_w_in, conv_k, conv_w_out, attn_w_in, attn_w_out,
           mlstm_w_in, mlstm_b_i, mlstm_b_f, mlstm_w_out):
    batch, seq, d = x.shape
    depth = norm_g.shape[0]
    h = x.reshape(batch * seq, d)
    dils = tuple(dil for _, dil in DILATION_PAIRS)
    width = ATT_HEADS * ATT_HD
    for i in range(depth):
        g = [norm_g[i, k][None, :] for k in range(norm_g.shape[1])]
        h = _ffn(h, g[0], g[1], ffn1_wg[i].astype(BF16), ffn1_wu[i].astype(BF16), ffn1_wd[i].astype(BF16))
        mixer, j = i % N_MIXERS, i // N_MIXERS
        if mixer == 0:
            h = _conv_mixer(h, g[2], g[3], conv_w_in[j].astype(BF16), conv_k[j],
                            conv_w_out[j].astype(BF16), seq=seq)
        elif mixer == 1:
            w_in = attn_w_in[j].astype(BF16)
            qkvs = [_qkv_proj(h, g[2], w_in, group=gi, width=width, batch=batch, seq=seq, dil=dil)
                    for gi, dil in enumerate(dils)]
            att = _attention(qkvs, batch=batch, seq=seq, dils=dils, width=width)
            h = _outproj(att.reshape(batch * seq, width), h, attn_w_out[j].astype(BF16), g[3])
        else:
            q, kt, v, og, gc, gr = _mlstm_proj(h, g[2], mlstm_w_in[j], mlstm_b_i[j], mlstm_b_f[j])
            hg = _mlstm_scan(q, kt, v, og, gc, gr, batch=batch, seq=seq)
            h = _outproj(hg, h, mlstm_w_out[j].astype(BF16), g[3])
        h = _ffn(h, g[4], g[5], ffn2_wg[i].astype(BF16), ffn2_wu[i].astype(BF16), ffn2_wd[i].astype(BF16))
    return h.reshape(batch, seq, d)
```

```python
import functools
import math

import jax
import jax.numpy as jnp
from jax import lax
from jax.experimental import pallas as pl
from jax.experimental.pallas import tpu as pltpu

LANES = 128
SUBLANES = 8
V7X_VMEM_BYTES = 64 * 1024 * 1024

EPS = 1e-6
CONV_W = 3
ATT_HEADS = 16
ATT_HD = 64
DILATION_PAIRS = ((128, 1), (512, 4), (2048, 16))
ATT_BLOCK = 128
ROPE_THETA = 500000.0
ROT_DIM = ATT_HD // 4
ML_HEADS = 4
ML_DQK = 128
ML_DV = 256
ML_CHUNK = 128
N_MIXERS = 3

BF16 = jnp.bfloat16
F32 = jnp.float32
NEG_INF = float("-inf")


def _params(semantics, vmem_bytes):
    return pltpu.CompilerParams(dimension_semantics=semantics,
                                vmem_limit_bytes=min(int(vmem_bytes), V7X_VMEM_BYTES - (4 << 20)))


def _rms(x32, g_row):
    ms = jnp.mean(x32 * x32, axis=-1, keepdims=True)
    return (x32 * lax.rsqrt(ms + EPS)) * g_row


def _dot(a, b):
    return jnp.dot(a, b, preferred_element_type=F32)


def _dot_nt(a, b):
    return lax.dot_general(a, b, (((1,), (1,)), ((), ())), preferred_element_type=F32)


def _dot_f32(a, b):
    return jnp.dot(a, b, preferred_element_type=F32, precision=lax.Precision.HIGHEST)


def _fill_bf16(jobs):
    chunks = []
    used = {}
    for src, dst, stage, sems in jobs:
        rc = stage.shape[1]
        for c in range(src.shape[0] // rc):
            slot = used.get(id(stage), 0) % 2
            used[id(stage)] = used.get(id(stage), 0) + 1
            rows = slice(c * rc, (c + 1) * rc)
            copy = pltpu.make_async_copy(src.at[rows, :], stage.at[slot], sems.at[slot])
            chunks.append((copy, dst, rows, stage, slot))
    chunks[0][0].start()
    for n, (copy, dst, rows, stage, slot) in enumerate(chunks):
        if n + 1 < len(chunks):
            chunks[n + 1][0].start()
        copy.wait()
        dst[rows, :] = stage[slot].astype(BF16)


def _ffn_kernel(x_ref, gpre_ref, gpost_ref, wg_hbm, wu_hbm, wd_hbm, o_ref,
                wg_ref, wu_ref, wd_ref, acc_ref, stage_in, stage_out, sem_in, sem_out, *, cw, layer):
    @pl.when(pl.program_id(0) == 0)
    def _():
        _fill_bf16([(wg_hbm.at[layer], wg_ref, stage_in, sem_in),
                    (wu_hbm.at[layer], wu_ref, stage_in, sem_in),
                    (wd_hbm.at[layer], wd_ref, stage_out, sem_out)])

    xn = _rms(x_ref[...], gpre_ref[...]).astype(BF16)
    ff = wg_ref.shape[1]
    for c in range(ff // cw):
        lo = c * cw
        g = _dot(xn, wg_ref[:, lo:lo + cw])
        u = _dot(xn, wu_ref[:, lo:lo + cw])
        h = ((g * jax.nn.sigmoid(g)) * u).astype(BF16)
        part = _dot(h, wd_ref[lo:lo + cw, :])
        if c == 0:
            acc_ref[...] = part
        else:
            acc_ref[...] += part
    o_ref[...] = x_ref[...] + 0.5 * _rms(acc_ref[...], gpost_ref[...])


def _resident(shape):
    return pl.BlockSpec(shape, lambda *_: (0,) * len(shape), pipeline_mode=pl.Buffered(1))


def _ffn(x, g_pre, g_post, wg, wu, wd, *, layer, tm=1024, cw=256, stage_rows=128):
    m, d = x.shape
    ff = wg.shape[2]
    tm = min(tm, m)
    out_rows = stage_rows * ff // d
    assert m % tm == 0 and ff % cw == 0 and d % stage_rows == 0 and ff % out_rows == 0
    vmem = (4 * tm * d * 4
            + 3 * d * ff * 2
            + 4 * stage_rows * ff * 4
            + tm * d * (2 + 4 + 4)
            + 2 * tm * cw * (4 + 4 + 2 + 4)
            + (4 << 20))
    hbm = pl.BlockSpec(memory_space=pl.ANY)
    return pl.pallas_call(
        functools.partial(_ffn_kernel, cw=cw, layer=layer),
        grid=(m // tm,),
        in_specs=[
            pl.BlockSpec((tm, d), lambda i: (i, 0)),
            _resident((1, d)), _resident((1, d)),
            hbm, hbm, hbm,
        ],
        out_specs=pl.BlockSpec((tm, d), lambda i: (i, 0)),
        out_shape=jax.ShapeDtypeStruct((m, d), F32),
        scratch_shapes=[
            pltpu.VMEM((d, ff), BF16), pltpu.VMEM((d, ff), BF16), pltpu.VMEM((ff, d), BF16),
            pltpu.VMEM((tm, d), F32),
            pltpu.VMEM((2, stage_rows, ff), F32), pltpu.VMEM((2, out_rows, d), F32),
            pltpu.SemaphoreType.DMA((2,)), pltpu.SemaphoreType.DMA((2,)),
        ],
        compiler_params=_params(("arbitrary",), vmem),
        name="ffn",
    )(x, g_pre, g_post, wg, wu, wd)


def _outproj_kernel(a_ref, x_ref, w_ref, g_ref, o_ref):
    y = _dot(a_ref[...], w_ref[...])
    o_ref[...] = x_ref[...] + _rms(y, g_ref[...])


def _outproj(a, x, w, g_post, *, tm=512):
    m, d = x.shape
    k = a.shape[1]
    tm = min(tm, m)
    assert m % tm == 0
    vmem = 2 * tm * k * 2 + 4 * tm * d * 4 + 2 * k * d * 2 + 2 * tm * d * 4 + (4 << 20)
    return pl.pallas_call(
        _outproj_kernel,
        grid=(m // tm,),
        in_specs=[
            pl.BlockSpec((tm, k), lambda i: (i, 0)),
            pl.BlockSpec((tm, d), lambda i: (i, 0)),
            pl.BlockSpec((k, d), lambda i: (0, 0)),
            pl.BlockSpec((1, d), lambda i: (0, 0)),
        ],
        out_specs=pl.BlockSpec((tm, d), lambda i: (i, 0)),
        out_shape=jax.ShapeDtypeStruct((m, d), F32),
        compiler_params=_params(("parallel",), vmem),
        name="outproj",
    )(a, x, w, g_post)


def _conv_kernel(x_ref, gpre_ref, gpost_ref, win_ref, k_ref, wout_ref, o_ref, ext_ref, *, tiles_per_seq):
    i = pl.program_id(0)
    tm, d = x_ref.shape
    halo = SUBLANES
    xn = _rms(x_ref[...], gpre_ref[...]).astype(BF16)
    p = _dot(xn, win_ref[...])
    cu = p[:, d:2 * d] * p[:, 2 * d:3 * d]

    @pl.when(i % tiles_per_seq == 0)
    def _():
        ext_ref[0:halo, :] = jnp.zeros((halo, d), F32)

    ext_ref[halo:halo + tm, :] = cu
    kk = k_ref[...]
    conv = (kk[0:1, :] * cu
            + kk[1:2, :] * ext_ref[halo - 1:halo - 1 + tm, :]
            + kk[2:3, :] * ext_ref[halo - 2:halo - 2 + tm, :])
    ext_ref[0:halo, :] = cu[tm - halo:tm, :]
    y = _dot((p[:, 0:d] * conv).astype(BF16), wout_ref[...])
    o_ref[...] = x_ref[...] + _rms(y, gpost_ref[...])


def _conv_mixer(x, g_pre, g_post, w_in, k, w_out, *, seq, tm=512):
    m, d = x.shape
    tm = min(tm, seq)
    assert seq % tm == 0 and m % seq == 0
    vmem = (4 * tm * d * 4 + 2 * (3 * d * d + d * d) * 2 + (tm + SUBLANES) * d * 4
            + tm * 3 * d * 4 + 6 * tm * d * 4 + (4 << 20))
    return pl.pallas_call(
        functools.partial(_conv_kernel, tiles_per_seq=seq // tm),
        grid=(m // tm,),
        in_specs=[
            pl.BlockSpec((tm, d), lambda i: (i, 0)),
            pl.BlockSpec((1, d), lambda i: (0, 0)),
            pl.BlockSpec((1, d), lambda i: (0, 0)),
            pl.BlockSpec((d, 3 * d), lambda i: (0, 0)),
            pl.BlockSpec((CONV_W, d), lambda i: (0, 0)),
            pl.BlockSpec((d, d), lambda i: (0, 0)),
        ],
        out_specs=pl.BlockSpec((tm, d), lambda i: (i, 0)),
        out_shape=jax.ShapeDtypeStruct((m, d), F32),
        scratch_shapes=[pltpu.VMEM((tm + SUBLANES, d), F32)],
        compiler_params=_params(("arbitrary",), vmem),
        name="conv_mixer",
    )(x, g_pre, g_post, w_in, k, w_out)


def _rope_tables(seq, dil, rows):
    length = seq // dil
    u = jnp.arange(rows, dtype=jnp.int32) % length
    pos = (u[None, :] * dil + jnp.arange(dil, dtype=jnp.int32)[:, None]).astype(F32)
    inv = ROPE_THETA ** (-jnp.arange(0, ROT_DIM, 2, dtype=F32) / ROT_DIM)
    ang = pos[:, :, None] * inv[None, None, :]
    cos, sin = jnp.cos(ang), jnp.sin(ang)
    half = ROT_DIM // 2
    dd = jnp.arange(LANES) % ATT_HD
    fi = dd % half
    cos_l = jnp.take(cos, fi, axis=-1)
    sin_l = jnp.take(sin, fi, axis=-1)
    c_tab = jnp.where(dd < ROT_DIM, cos_l, 1.0)
    s_lo = jnp.where(dd < half, -sin_l, 0.0)
    s_hi = jnp.where((dd >= half) & (dd < ROT_DIM), sin_l, 0.0)
    return c_tab.astype(F32), s_lo.astype(F32), s_hi.astype(F32)


def _qkv_kernel(x_ref, g_ref, w_ref, c_ref, slo_ref, shi_ref, o_ref, slab_ref, xp_ref, *, width, dil):
    tm, d = x_ref.shape
    per = tm // dil
    xn = _rms(x_ref[...], g_ref[...])
    if dil == 1:
        xp = xn.astype(BF16)
    else:
        for cb in range(d // LANES):
            slab_ref[cb] = xn[:, cb * LANES:(cb + 1) * LANES]
        for r in range(dil):
            for cb in range(d // LANES):
                piece = slab_ref[cb, pl.ds(r, per, stride=dil), :]
                xp_ref[r * per:(r + 1) * per, cb * LANES:(cb + 1) * LANES] = piece.astype(BF16)
        xp = xp_ref[...]
    p = _dot(xp, w_ref[...])
    c_tab = c_ref[...].reshape(tm, LANES)
    s_lo = slo_ref[...].reshape(tm, LANES)
    s_hi = shi_ref[...].reshape(tm, LANES)
    half = ROT_DIM // 2
    scale = ATT_HD ** -0.5
    for part in range(2):
        for cb in range(width // LANES):
            lo = part * width + cb * LANES
            blk = p[:, lo:lo + LANES]
            rot = (blk * c_tab
                   + pltpu.roll(blk, LANES - half, axis=1) * s_lo
                   + pltpu.roll(blk, half, axis=1) * s_hi)
            if part == 0:
                rot = rot * scale
            o_ref[:, :, lo:lo + LANES] = rot.astype(BF16).reshape(dil, per, LANES)
    o_ref[:, :, 2 * width:3 * width] = p[:, 2 * width:3 * width].astype(BF16).reshape(dil, per, width)


def _qkv_proj(x, g_pre, w, *, group, width, batch, seq, dil, tm=512):
    m, d = x.shape
    tm = min(tm, seq)
    per = tm // dil
    assert seq % tm == 0 and tm % dil == 0 and per % (2 * SUBLANES) == 0
    c_tab, s_lo, s_hi = _rope_tables(seq, dil, seq // dil)
    tiles_per_seq = seq // tm
    tab_spec = pl.BlockSpec((dil, per, LANES), lambda i: (0, i % tiles_per_seq, 0))
    vmem = (2 * tm * d * 4 + d * 3 * width * 2 + 2 * tm * 3 * width * 2 + 6 * tm * LANES * 4
            + tm * 3 * width * 4 + tm * d * (4 + 4 + 2) + 4 * tm * width * 4 + (4 << 20))
    return pl.pallas_call(
        functools.partial(_qkv_kernel, width=width, dil=dil),
        grid=(m // tm,),
        in_specs=[
            pl.BlockSpec((tm, d), lambda i: (i, 0)),
            _resident((1, d)),
            pl.BlockSpec((d, 3 * width), lambda i: (0, group), pipeline_mode=pl.Buffered(1)),
            tab_spec, tab_spec, tab_spec,
        ],
        out_specs=pl.BlockSpec((dil, per, 3 * width), lambda i: (0, i, 0)),
        out_shape=jax.ShapeDtypeStruct((dil, m // dil, 3 * width), BF16),
        scratch_shapes=[pltpu.VMEM((d // LANES, tm, LANES), F32), pltpu.VMEM((tm, d), BF16)],
        compiler_params=_params(("parallel",), vmem),
        name=f"qkv_proj_d{dil}",
    )(x, g_pre, w, c_tab, s_lo, s_hi)


def _bdot_nt(a, b):
    return lax.dot_general(a, b, (((2,), (2,)), ((0,), (0,))), preferred_element_type=F32)


def _bdot(a, b):
    return lax.dot_general(a, b, (((2,), (1,)), ((0,), (0,))), preferred_element_type=F32)


def _value_operands(v):
    hi = lax.broadcasted_iota(jnp.int32, v.shape, 2).astype(F32).astype(v.dtype) >= ATT_HD
    one = jnp.ones_like(v)
    return jnp.where(hi, one, v), jnp.where(hi, v, one)


def _attn_blocks(q, kc, voc, kp, vop, first_ok):
    g, blk, _ = q.shape
    shape2 = (g, 2 * blk, LANES)
    row2 = lax.broadcasted_iota(jnp.int32, shape2, 1)
    lane = lax.broadcasted_iota(jnp.int32, shape2, 2)
    row = jnp.bitwise_and(row2, blk - 1)
    q2 = jnp.concatenate([q, q], axis=1)
    qs = jnp.where((row2 >= blk) == (lane >= ATT_HD), q2, jnp.zeros_like(q2))
    s_c = jnp.where(lane <= row, _bdot_nt(qs, kc), NEG_INF)
    if kp is not None:
        gidx = lax.broadcasted_iota(jnp.int32, shape2, 0)
        keep_prev = jnp.logical_and(lane >= row, jnp.logical_or(gidx > 0, first_ok))
        s_p = jnp.where(keep_prev, _bdot_nt(qs, kp), NEG_INF)
        mx = jnp.max(jnp.maximum(s_c, s_p), axis=2, keepdims=True)
        p_p = jnp.exp(s_p - mx).astype(BF16)
    else:
        mx = jnp.max(s_c, axis=2, keepdims=True)
    p_c = jnp.exp(s_c - mx).astype(BF16)
    res = []
    for hh in range(2):
        rows = slice(hh * blk, (hh + 1) * blk)
        if kp is not None:
            res.append(_bdot(jnp.concatenate([p_c[:, rows], p_p[:, rows]], axis=2),
                             jnp.concatenate([voc[hh], vop[hh]], axis=1)))
        else:
            res.append(_bdot(p_c[:, rows], voc[hh]))
    hi = lax.broadcasted_iota(jnp.int32, q.shape, 2) >= ATT_HD
    return (jnp.where(hi, res[1], res[0]), jnp.where(hi, res[0], res[1]),
            jnp.where(hi, mx[:, blk:], mx[:, :blk]))


def _attn_kernel(*refs, dils, seq, gb):
    n_g = len(dils)
    qkv = refs[:3 * n_g]
    o_ref = refs[3 * n_g]
    num_ref, den_ref, max_ref = refs[3 * n_g + 1:3 * n_g + 4]
    blk = ATT_BLOCK
    span = gb * blk

    for g, dil in enumerate(dils):
        q_ref, k_ref, v_ref = qkv[3 * g:3 * g + 3]
        nb = seq // dil // blk
        assert nb == 1 or nb % gb == 0

        def body(c, carry, q_ref=q_ref, k_ref=k_ref, v_ref=v_ref, nb=nb, dil=dil, g=g):
            if nb == 1:
                res = pl.ds(c * gb, gb)
                num, den, mx = _attn_blocks(q_ref[res], k_ref[res], _value_operands(v_ref[res]), None, None, None)
                for t in range(gb):
                    dst = pl.ds(c * gb + t, blk, stride=dil)
                    num_ref[g, dst, :] = num[t]
                    den_ref[g, dst, :] = den[t]
                    max_ref[g, dst, :] = mx[t]
                return carry
            r = (c * gb) // nb
            n0 = (c * gb) % nb
            cur = pl.ds(pl.multiple_of(n0 * blk, blk), span)
            first = pl.ds(pl.multiple_of(jnp.maximum(n0 - 1, 0) * blk, blk), blk)
            q = q_ref[r, cur, :].reshape(gb, blk, LANES)
            kc = k_ref[r, cur, :].reshape(gb, blk, LANES)
            voc = _value_operands(v_ref[r, cur, :].reshape(gb, blk, LANES))
            vof = _value_operands(v_ref[r, first, :][None])
            kp = jnp.concatenate([k_ref[r, first, :][None], kc[:gb - 1]], axis=0)
            vop = [jnp.concatenate([vof[hh], voc[hh][:gb - 1]], axis=0) for hh in range(2)]
            num, den, mx = _attn_blocks(q, kc, voc, kp, vop, n0 > 0)
            start = r + dil * n0 * blk
            if dil == 1:
                dst = pl.ds(pl.multiple_of(start, blk), span)
            else:
                dst = pl.ds(start, span, stride=dil)
            num_ref[g, dst, :] = num.reshape(span, LANES)
            den_ref[g, dst, :] = den.reshape(span, LANES)
            max_ref[g, dst, :] = mx.reshape(span, LANES)
            return carry

        lax.fori_loop(0, seq // span, body, 0, unroll=True)

    rows = 2 * blk

    def mix(t, carry):
        sl = pl.ds(pl.multiple_of(t * rows, rows), rows)
        ms = [max_ref[g, sl, :] for g in range(n_g)]
        top = functools.reduce(jnp.maximum, ms)
        ws = [jnp.exp(m - top) for m in ms]
        total = functools.reduce(jnp.add, [ws[g] * pltpu.roll(den_ref[g, sl, :], ATT_HD, axis=1)
                                           for g in range(n_g)])
        inv = 1.0 / total
        acc = functools.reduce(jnp.add, [(ws[g] * inv) * num_ref[g, sl, :] for g in range(n_g)])
        o_ref[sl, :] = acc.astype(BF16)
        return carry

    lax.fori_loop(0, seq // rows, mix, 0)


def _attention(qkvs, *, batch, seq, dils, width, gb=4):
    n_pairs = width // LANES
    args, specs = [], []
    for arr, dil in zip(qkvs, dils):
        length = seq // dil
        for part in range(3):
            args.append(arr)
            specs.append(pl.BlockSpec((dil, length, LANES),
                                      lambda b, hp, part=part: (0, b, part * n_pairs + hp)))
    n_g = len(dils)
    vmem = (2 * 3 * n_g * seq * LANES * 2 + 3 * n_g * seq * LANES * 4 + 2 * seq * LANES * 2
            + 32 * gb * ATT_BLOCK * LANES * 4 + (8 << 20))
    return pl.pallas_call(
        functools.partial(_attn_kernel, dils=tuple(dils), seq=seq, gb=gb),
        grid=(batch, n_pairs),
        in_specs=specs,
        out_specs=pl.BlockSpec((None, seq, LANES), lambda b, hp: (b, 0, hp)),
        out_shape=jax.ShapeDtypeStruct((batch, seq, width), BF16),
        scratch_shapes=[pltpu.VMEM((n_g, seq, LANES), F32)] * 3,
        compiler_params=_params(("parallel", "parallel"), vmem),
        name="dilated_attention",
    )(*args)


def _log_sigmoid(z):
    return jnp.minimum(z, 0.0) - jnp.log1p(jnp.exp(-jnp.abs(z)))


def _mlstm_proj_kernel(x_ref, g_ref, wq_ref, wkt_ref, wv_ref, wo_ref, wgc_ref, wgr_ref, bc_ref, br_ref,
                       q_ref, kt_ref, v_ref, og_ref, gc_ref, gr_ref):
    xn = _rms(x_ref[...], g_ref[...]).astype(BF16)
    q_ref[...] = _dot(xn, wq_ref[...]).astype(BF16)
    kt_ref[...] = (_dot_nt(wkt_ref[...], xn) * (ML_DQK ** -0.5)).astype(BF16)
    v_ref[...] = _dot(xn, wv_ref[...]).astype(BF16)
    og_ref[...] = _dot(xn, wo_ref[...])
    zc = _dot(xn, wgc_ref[...]) + bc_ref[...]
    lane = lax.broadcasted_iota(jnp.int32, zc.shape, 1)
    gc_ref[...] = jnp.where(lane >= ML_HEADS, _log_sigmoid(zc), zc)
    zr = _dot_nt(wgr_ref[...], xn) + br_ref[...]
    rowi = lax.broadcasted_iota(jnp.int32, zr.shape, 0)
    gr_ref[...] = jnp.where(rowi >= ML_HEADS, _log_sigmoid(zr), zr)


def _mlstm_proj(x, g_pre, w_in, b_i, b_f, *, tm=512):
    m, d = x.shape
    nh = ML_HEADS
    o1 = nh * ML_DQK
    o2 = 2 * o1
    o3 = o2 + nh * ML_DV
    o4 = o3 + nh * ML_DV
    wq = w_in[:, :o1].astype(BF16)
    wkt = w_in[:, o1:o2].T.astype(BF16)
    wv = w_in[:, o2:o3].astype(BF16)
    wo = w_in[:, o3:o4].astype(BF16)
    wg = w_in[:, o4:]
    wgc = jnp.pad(wg, ((0, 0), (0, LANES - 2 * nh))).astype(BF16)
    wgr = wg.T.astype(BF16)
    bias = jnp.concatenate([b_i, b_f]).astype(F32)
    bc = jnp.pad(bias, (0, LANES - 2 * nh))[None, :]
    br = bias[:, None]
    tm = min(tm, m)
    assert m % tm == 0 and 2 * nh == SUBLANES
    dv = nh * ML_DV
    vmem = (2 * tm * d * 4 + 2 * d * (2 * o1 + 2 * dv + LANES + SUBLANES) * 2
            + 2 * tm * (2 * o1 * 2 + dv * 2 + dv * 4 + LANES * 4 + SUBLANES * 4)
            + tm * (2 * o1 + 2 * dv) * 4 + tm * d * 6 + (4 << 20))
    full = lambda shape: pl.BlockSpec(shape, lambda i: (0, 0))
    return pl.pallas_call(
        _mlstm_proj_kernel,
        grid=(m // tm,),
        in_specs=[
            pl.BlockSpec((tm, d), lambda i: (i, 0)),
            full((1, d)), full((d, o1)), full((o1, d)), full((d, dv)), full((d, dv)),
            full((d, LANES)), full((2 * nh, d)), full((1, LANES)), full((2 * nh, 1)),
        ],
        out_specs=[
            pl.BlockSpec((tm, o1), lambda i: (i, 0)),
            pl.BlockSpec((o1, tm), lambda i: (0, i)),
            pl.BlockSpec((tm, dv), lambda i: (i, 0)),
            pl.BlockSpec((tm, dv), lambda i: (i, 0)),
            pl.BlockSpec((tm, LANES), lambda i: (i, 0)),
            pl.BlockSpec((2 * nh, tm), lambda i: (0, i)),
        ],
        out_shape=[
            jax.ShapeDtypeStruct((m, o1), BF16),
            jax.ShapeDtypeStruct((o1, m), BF16),
            jax.ShapeDtypeStruct((m, dv), BF16),
            jax.ShapeDtypeStruct((m, dv), F32),
            jax.ShapeDtypeStruct((m, LANES), F32),
            jax.ShapeDtypeStruct((2 * nh, m), F32),
        ],
        compiler_params=_params(("parallel",), vmem),
        name="mlstm_proj",
    )(x, g_pre, wq, wkt, wv, wo, wgc, wgr, bc, br)


def _mlstm_kernel(q_ref, kt_ref, v_ref, og_ref, gc_ref, gr_ref, o_ref, st_ref, *, seq):
    nh, lc, dqk, dv = ML_HEADS, ML_CHUNK, ML_DQK, ML_DV
    st_ref[...] = jnp.zeros(st_ref.shape, F32)
    row = lax.broadcasted_iota(jnp.int32, (lc, lc), 0)
    col = lax.broadcasted_iota(jnp.int32, (lc, lc), 1)
    causal = col <= row
    tril = causal.astype(F32)
    triu = (row <= col).astype(F32)
    lane0 = lax.broadcasted_iota(jnp.int32, (lc, LANES), 1) == 0

    def body(j, ms):
        pos = pl.ds(pl.multiple_of(j * lc, lc), lc)
        gr_c = gr_ref[:, pos]
        gc_c = gc_ref[pos, :]
        b_rows = _dot_f32(gr_c, triu)
        b_cols = _dot_f32(tril, gc_c)
        new_ms = []
        for h in range(nh):
            m_h = ms[h]
            g_row = gr_c[h:h + 1, :] - b_rows[nh + h:nh + h + 1, :]
            b_col = b_cols[:, nh + h:nh + h + 1]
            g_col = gc_c[:, h:h + 1] - b_col
            dmat = jnp.where(causal, g_row, NEG_INF)
            mu = jnp.maximum(m_h, jnp.max(dmat, axis=1, keepdims=True))
            w = jnp.exp(dmat - mu)
            wi = jnp.exp(m_h - mu)
            emt = jnp.exp(-(b_col + mu))
            qh = q_ref[pos, h * dqk:(h + 1) * dqk]
            kth = kt_ref[h * dqk:(h + 1) * dqk, pos]
            vh = v_ref[pos, h * dv:(h + 1) * dv]
            a = _dot(qh, kth) * w
            state = st_ref[h]
            qc = _dot(qh, state.astype(BF16))
            num = _dot(a.astype(BF16), vh) + wi * qc[:, :dv]
            den = jnp.sum(a, axis=1, keepdims=True) + wi * qc[:, dv:dv + 1]
            hval = num / jnp.maximum(jnp.abs(den), emt)
            gate = jax.nn.sigmoid(og_ref[pos, h * dv:(h + 1) * dv])
            o_ref[pos, h * dv:(h + 1) * dv] = (hval * gate).astype(BF16)
            mu_last = mu[lc - 1:lc, :]
            we = jnp.exp(g_col - mu_last)
            decay = jnp.exp(m_h - mu_last)
            vaug = jnp.concatenate([vh.astype(F32) * we, jnp.where(lane0, we, 0.0)], axis=1)
            st_ref[h] = decay * state + _dot(kth, vaug.astype(BF16))
            new_ms.append(b_col[lc - 1:lc, :] + mu_last)
        return tuple(new_ms)

    lax.fori_loop(0, seq // lc, body, tuple(jnp.zeros((1, 1), F32) for _ in range(nh)))


def _mlstm_scan(q, kt, v, og, gc, gr, *, batch, seq):
    nh, dqk, dv = ML_HEADS, ML_DQK, ML_DV
    assert seq % ML_CHUNK == 0
    vmem = (2 * seq * (2 * nh * dqk * 2 + nh * dv * 2 + nh * dv * 4 + LANES * 4 + SUBLANES * 4 + nh * dv * 2)
            + nh * dqk * (dv + LANES) * 4 + (8 << 20))
    return pl.pallas_call(
        functools.partial(_mlstm_kernel, seq=seq),
        grid=(batch,),
        in_specs=[
            pl.BlockSpec((seq, nh * dqk), lambda b: (b, 0)),
            pl.BlockSpec((nh * dqk, seq), lambda b: (0, b)),
            pl.BlockSpec((seq, nh * dv), lambda b: (b, 0)),
            pl.BlockSpec((seq, nh * dv), lambda b: (b, 0)),
            pl.BlockSpec((seq, LANES), lambda b: (b, 0)),
            pl.BlockSpec((2 * nh, seq), lambda b: (0, b)),
        ],
        out_specs=pl.BlockSpec((seq, nh * dv), lambda b: (b, 0)),
        out_shape=jax.ShapeDtypeStruct((batch * seq, nh * dv), BF16),
        scratch_shapes=[pltpu.VMEM((nh, dqk, dv + LANES), F32)],
        compiler_params=_params(("parallel",), vmem),
        name="mlstm_scan",
    )(q, kt, v, og, gc, gr)


def kernel(x, norm_g, ffn1_wg, ffn1_wu, ffn1_wd, ffn2_wg, ffn2_wu, ffn2_wd,
           conv_w_in, conv_k, conv_w_out, attn_w_in, attn_w_out,
           mlstm_w_in, mlstm_b_i, mlstm_b_f, mlstm_w_out):
    batch, seq, d = x.shape
    depth = norm_g.shape[0]
    h = x.reshape(batch * seq, d)
    dils = tuple(dil for _, dil in DILATION_PAIRS)
    width = ATT_HEADS * ATT_HD
    for i in range(depth):
        g = [norm_g[i, k][None, :] for k in range(norm_g.shape[1])]
        h = _ffn(h, g[0], g[1], ffn1_wg, ffn1_wu, ffn1_wd, layer=i)
        mixer, j = i % N_MIXERS, i // N_MIXERS
        if mixer == 0:
            h = _conv_mixer(h, g[2], g[3], conv_w_in[j].astype(BF16), conv_k[j],
                            conv_w_out[j].astype(BF16), seq=seq)
        elif mixer == 1:
            w_in = attn_w_in[j].astype(BF16)
            qkvs = [_qkv_proj(h, g[2], w_in, group=gi, width=width, batch=batch, seq=seq, dil=dil)
                    for gi, dil in enumerate(dils)]
            att = _attention(qkvs, batch=batch, seq=seq, dils=dils, width=width)
            h = _outproj(att.reshape(batch * seq, width), h, attn_w_out[j].astype(BF16), g[3])
        else:
            q, kt, v, og, gc, gr = _mlstm_proj(h, g[2], mlstm_w_in[j], mlstm_b_i[j], mlstm_b_f[j])
            hg = _mlstm_scan(q, kt, v, og, gc, gr, batch=batch, seq=seq)
            h = _outproj(hg, h, mlstm_w_out[j].astype(BF16), g[3])
        h = _ffn(h, g[4], g[5], ffn2_wg, ffn2_wu, ffn2_wd, layer=i)
    return h.reshape(batch, seq, d)
```

```python
import functools
import math

import jax
import jax.numpy as jnp
from jax import lax
from jax.experimental import pallas as pl
from jax.experimental.pallas import tpu as pltpu

LANES = 128
SUBLANES = 8
V7X_VMEM_BYTES = 64 * 1024 * 1024

EPS = 1e-6
CONV_W = 3
ATT_HEADS = 16
ATT_HD = 64
DILATION_PAIRS = ((128, 1), (512, 4), (2048, 16))
ATT_BLOCK = 128
ROPE_THETA = 500000.0
ROT_DIM = ATT_HD // 4
ML_HEADS = 4
ML_DQK = 128
ML_DV = 256
ML_CHUNK = 128
N_MIXERS = 3

BF16 = jnp.bfloat16
F32 = jnp.float32
NEG_INF = float("-inf")


def _params(semantics, vmem_bytes):
    return pltpu.CompilerParams(dimension_semantics=semantics,
                                vmem_limit_bytes=min(int(vmem_bytes), V7X_VMEM_BYTES - (4 << 20)))


def _rms(x32, g_row):
    ms = jnp.mean(x32 * x32, axis=-1, keepdims=True)
    return (x32 * lax.rsqrt(ms + EPS)) * g_row


def _dot(a, b):
    return jnp.dot(a, b, preferred_element_type=F32)


def _dot_nt(a, b):
    return lax.dot_general(a, b, (((1,), (1,)), ((), ())), preferred_element_type=F32)


def _dot_f32(a, b):
    return jnp.dot(a, b, preferred_element_type=F32, precision=lax.Precision.HIGHEST)


def _fill_bf16(jobs):
    chunks = []
    used = {}
    for src, dst, stage, sems in jobs:
        rc = stage.shape[1]
        for c in range(src.shape[0] // rc):
            slot = used.get(id(stage), 0) % 2
            used[id(stage)] = used.get(id(stage), 0) + 1
            rows = slice(c * rc, (c + 1) * rc)
            copy = pltpu.make_async_copy(src.at[rows, :], stage.at[slot], sems.at[slot])
            chunks.append((copy, dst, rows, stage, slot))
    chunks[0][0].start()
    for n, (copy, dst, rows, stage, slot) in enumerate(chunks):
        if n + 1 < len(chunks):
            chunks[n + 1][0].start()
        copy.wait()
        dst[rows, :] = stage[slot].astype(BF16)


def _ffn_kernel(*refs, cw, layer, mix_layer):
    if mix_layer is None:
        (x_ref, gpre_ref, gpost_ref, wg_hbm, wu_hbm, wd_hbm, o_ref,
         wg_ref, wu_ref, wd_ref, acc_ref, stage_in, stage_out, sem_in, sem_out) = refs
        jobs = []
    else:
        (x_ref, gpre_ref, gpost_ref, wg_hbm, wu_hbm, wd_hbm, a_ref, wmix_hbm, gmix_ref, o_ref,
         wg_ref, wu_ref, wd_ref, acc_ref, stage_in, stage_out, sem_in, sem_out,
         wmix_ref, stage_mix, sem_mix) = refs
        jobs = [(wmix_hbm.at[mix_layer], wmix_ref, stage_mix, sem_mix)]

    @pl.when(pl.program_id(0) == 0)
    def _():
        _fill_bf16(jobs + [(wg_hbm.at[layer], wg_ref, stage_in, sem_in),
                           (wu_hbm.at[layer], wu_ref, stage_in, sem_in),
                           (wd_hbm.at[layer], wd_ref, stage_out, sem_out)])

    if mix_layer is None:
        x_in = x_ref
    else:
        o_ref[...] = x_ref[...] + _rms(_dot(a_ref[...], wmix_ref[...]), gmix_ref[...])
        x_in = o_ref
    xn = _rms(x_in[...], gpre_ref[...]).astype(BF16)
    ff = wg_ref.shape[1]
    for c in range(ff // cw):
        lo = c * cw
        g = _dot(xn, wg_ref[:, lo:lo + cw])
        u = _dot(xn, wu_ref[:, lo:lo + cw])
        h = ((g * jax.nn.sigmoid(g)) * u).astype(BF16)
        part = _dot(h, wd_ref[lo:lo + cw, :])
        if c == 0:
            acc_ref[...] = part
        else:
            acc_ref[...] += part
    o_ref[...] = x_in[...] + 0.5 * _rms(acc_ref[...], gpost_ref[...])


def _resident(shape):
    return pl.BlockSpec(shape, lambda *_: (0,) * len(shape), pipeline_mode=pl.Buffered(1))


def _ffn(x, g_pre, g_post, wg, wu, wd, *, layer, mix=None, tm=1024, cw=256, stage_rows=64):
    m, d = x.shape
    ff = wg.shape[2]
    tm = min(tm, m)
    out_rows = stage_rows * ff // d
    assert m % tm == 0 and ff % cw == 0 and d % stage_rows == 0 and ff % out_rows == 0
    vmem = (4 * tm * d * 4
            + 3 * d * ff * 2
            + 4 * stage_rows * ff * 4
            + tm * d * (2 + 4 + 4)
            + 2 * tm * cw * (4 + 4 + 2 + 4)
            + (4 << 20))
    hbm = pl.BlockSpec(memory_space=pl.ANY)
    args = [x, g_pre, g_post, wg, wu, wd]
    in_specs = [pl.BlockSpec((tm, d), lambda i: (i, 0)), _resident((1, d)), _resident((1, d)), hbm, hbm, hbm]
    scratch = [
        pltpu.VMEM((d, ff), BF16), pltpu.VMEM((d, ff), BF16), pltpu.VMEM((ff, d), BF16),
        pltpu.VMEM((tm, d), F32),
        pltpu.VMEM((2, stage_rows, ff), F32), pltpu.VMEM((2, out_rows, d), F32),
        pltpu.SemaphoreType.DMA((2,)), pltpu.SemaphoreType.DMA((2,)),
    ]
    mix_layer = None
    if mix is not None:
        a, w_mix, mix_layer, g_mix = mix
        k = a.shape[1]
        assert k % stage_rows == 0
        args += [a, w_mix, g_mix]
        in_specs += [pl.BlockSpec((tm, k), lambda i: (i, 0)), hbm, _resident((1, d))]
        scratch += [pltpu.VMEM((k, d), BF16), pltpu.VMEM((2, stage_rows, d), F32), pltpu.SemaphoreType.DMA((2,))]
        vmem += 2 * tm * k * 2 + k * d * 2 + 2 * stage_rows * d * 4 + tm * d * 4
    return pl.pallas_call(
        functools.partial(_ffn_kernel, cw=cw, layer=layer, mix_layer=mix_layer),
        grid=(m // tm,),
        in_specs=in_specs,
        out_specs=pl.BlockSpec((tm, d), lambda i: (i, 0)),
        out_shape=jax.ShapeDtypeStruct((m, d), F32),
        scratch_shapes=scratch,
        compiler_params=_params(("arbitrary",), vmem),
        name="ffn" if mix is None else "outproj_ffn",
    )(*args)


def _conv_kernel(x_ref, gpre_ref, gpost_ref, win_ref, k_ref, wout_ref, o_ref, ext_ref, *, tiles_per_seq, nsub):
    i = pl.program_id(0)
    tm, d = x_ref.shape
    halo = SUBLANES
    sub = tm // nsub

    @pl.when(i % tiles_per_seq == 0)
    def _():
        ext_ref[0:halo, :] = jnp.zeros((halo, d), F32)

    kk = k_ref[...]
    for s in range(nsub):
        lo = s * sub
        xs = x_ref[lo:lo + sub, :]
        xn = _rms(xs, gpre_ref[...]).astype(BF16)
        p = _dot(xn, win_ref[...])
        cu = p[:, d:2 * d] * p[:, 2 * d:3 * d]
        ext_ref[halo + lo:halo + lo + sub, :] = cu
        conv = (kk[0:1, :] * cu
                + kk[1:2, :] * ext_ref[halo - 1 + lo:halo - 1 + lo + sub, :]
                + kk[2:3, :] * ext_ref[halo - 2 + lo:halo - 2 + lo + sub, :])
        y = _dot((p[:, 0:d] * conv).astype(BF16), wout_ref[...])
        o_ref[lo:lo + sub, :] = xs + _rms(y, gpost_ref[...])
    ext_ref[0:halo, :] = ext_ref[tm:tm + halo, :]


def _conv_mixer(x, g_pre, g_post, w_in, k, w_out, *, seq, tm=1024, nsub=2):
    m, d = x.shape
    tm = min(tm, seq)
    assert seq % tm == 0 and m % seq == 0 and tm % nsub == 0
    sub = tm // nsub
    vmem = (4 * tm * d * 4 + (3 * d * d + d * d) * 2 + (tm + SUBLANES) * d * 4
            + nsub * (sub * 3 * d * 4 + 6 * sub * d * 4) + (4 << 20))
    return pl.pallas_call(
        functools.partial(_conv_kernel, tiles_per_seq=seq // tm, nsub=nsub),
        grid=(m // tm,),
        in_specs=[
            pl.BlockSpec((tm, d), lambda i: (i, 0)),
            _resident((1, d)),
            _resident((1, d)),
            _resident((d, 3 * d)),
            _resident((CONV_W, d)),
            _resident((d, d)),
        ],
        out_specs=pl.BlockSpec((tm, d), lambda i: (i, 0)),
        out_shape=jax.ShapeDtypeStruct((m, d), F32),
        scratch_shapes=[pltpu.VMEM((tm + SUBLANES, d), F32)],
        compiler_params=_params(("arbitrary",), vmem),
        name="conv_mixer",
    )(x, g_pre, g_post, w_in, k, w_out)


def _rope_tables(seq, dil, rows):
    length = seq // dil
    u = jnp.arange(rows, dtype=jnp.int32) % length
    pos = (u[None, :] * dil + jnp.arange(dil, dtype=jnp.int32)[:, None]).astype(F32)
    inv = ROPE_THETA ** (-jnp.arange(0, ROT_DIM, 2, dtype=F32) / ROT_DIM)
    ang = pos[:, :, None] * inv[None, None, :]
    cos, sin = jnp.cos(ang), jnp.sin(ang)
    half = ROT_DIM // 2
    dd = jnp.arange(LANES) % ATT_HD
    fi = dd % half
    cos_l = jnp.take(cos, fi, axis=-1)
    sin_l = jnp.take(sin, fi, axis=-1)
    c_tab = jnp.where(dd < ROT_DIM, cos_l, 1.0)
    s_lo = jnp.where(dd < half, -sin_l, 0.0)
    s_hi = jnp.where((dd >= half) & (dd < ROT_DIM), sin_l, 0.0)
    return c_tab.astype(F32), s_lo.astype(F32), s_hi.astype(F32)


def _qkv_kernel(x_ref, g_ref, w_ref, c_ref, slo_ref, shi_ref, o_ref, slab_ref, xp_ref, *, width, dil):
    tm, d = x_ref.shape
    per = tm // dil
    xn = _rms(x_ref[...], g_ref[...])
    if dil == 1:
        xp = xn.astype(BF16)
    else:
        for cb in range(d // LANES):
            slab_ref[cb] = xn[:, cb * LANES:(cb + 1) * LANES]
        for r in range(dil):
            for cb in range(d // LANES):
                piece = slab_ref[cb, pl.ds(r, per, stride=dil), :]
                xp_ref[r * per:(r + 1) * per, cb * LANES:(cb + 1) * LANES] = piece.astype(BF16)
        xp = xp_ref[...]
    p = _dot(xp, w_ref[...])
    c_tab = c_ref[...].reshape(tm, LANES)
    s_lo = slo_ref[...].reshape(tm, LANES)
    s_hi = shi_ref[...].reshape(tm, LANES)
    half = ROT_DIM // 2
    scale = ATT_HD ** -0.5
    for part in range(2):
        for cb in range(width // LANES):
            lo = part * width + cb * LANES
            blk = p[:, lo:lo + LANES]
            rot = (blk * c_tab
                   + pltpu.roll(blk, LANES - half, axis=1) * s_lo
                   + pltpu.roll(blk, half, axis=1) * s_hi)
            if part == 0:
                rot = rot * scale
            o_ref[:, :, lo:lo + LANES] = rot.astype(BF16).reshape(dil, per, LANES)
    o_ref[:, :, 2 * width:3 * width] = p[:, 2 * width:3 * width].astype(BF16).reshape(dil, per, width)


def _qkv_proj(x, g_pre, w, *, group, width, batch, seq, dil, tm=512):
    m, d = x.shape
    tm = min(tm, seq)
    per = tm // dil
    assert seq % tm == 0 and tm % dil == 0 and per % (2 * SUBLANES) == 0
    c_tab, s_lo, s_hi = _rope_tables(seq, dil, seq // dil)
    tiles_per_seq = seq // tm
    tab_spec = pl.BlockSpec((dil, per, LANES), lambda i: (0, i % tiles_per_seq, 0))
    vmem = (2 * tm * d * 4 + d * 3 * width * 2 + 2 * tm * 3 * width * 2 + 6 * tm * LANES * 4
            + tm * 3 * width * 4 + tm * d * (4 + 4 + 2) + 4 * tm * width * 4 + (4 << 20))
    return pl.pallas_call(
        functools.partial(_qkv_kernel, width=width, dil=dil),
        grid=(m // tm,),
        in_specs=[
            pl.BlockSpec((tm, d), lambda i: (i, 0)),
            _resident((1, d)),
            pl.BlockSpec((d, 3 * width), lambda i: (0, group), pipeline_mode=pl.Buffered(1)),
            tab_spec, tab_spec, tab_spec,
        ],
        out_specs=pl.BlockSpec((dil, per, 3 * width), lambda i: (0, i, 0)),
        out_shape=jax.ShapeDtypeStruct((dil, m // dil, 3 * width), BF16),
        scratch_shapes=[pltpu.VMEM((d // LANES, tm, LANES), F32), pltpu.VMEM((tm, d), BF16)],
        compiler_params=_params(("parallel",), vmem),
        name=f"qkv_proj_d{dil}",
    )(x, g_pre, w, c_tab, s_lo, s_hi)


def _bdot_nt(a, b):
    return lax.dot_general(a, b, (((2,), (2,)), ((0,), (0,))), preferred_element_type=F32)


def _bdot(a, b):
    return lax.dot_general(a, b, (((2,), (1,)), ((0,), (0,))), preferred_element_type=F32)


def _value_operands(v):
    hi = lax.broadcasted_iota(jnp.int32, v.shape, 2).astype(F32).astype(v.dtype) >= ATT_HD
    one = jnp.ones_like(v)
    return jnp.where(hi, one, v), jnp.where(hi, v, one)


def _attn_blocks(q, kc, voc, kp, vop, phase, period):
    g, blk, _ = q.shape
    shape2 = (g, 2 * blk, LANES)
    row2 = lax.broadcasted_iota(jnp.int32, shape2, 1)
    lane = lax.broadcasted_iota(jnp.int32, shape2, 2)
    row = jnp.bitwise_and(row2, blk - 1)
    q2 = jnp.concatenate([q, q], axis=1)
    qs = jnp.where((row2 >= blk) == (lane >= ATT_HD), q2, jnp.zeros_like(q2))
    s_c = jnp.where(lane <= row, _bdot_nt(qs, kc), NEG_INF)
    if kp is not None:
        gidx = lax.broadcasted_iota(jnp.int32, shape2, 0)
        keep_prev = jnp.logical_and(lane >= row, jnp.bitwise_and(gidx + phase, period - 1) != 0)
        s_p = jnp.where(keep_prev, _bdot_nt(qs, kp), NEG_INF)
        mx = jnp.max(jnp.maximum(s_c, s_p), axis=2, keepdims=True)
        p_p = jnp.exp(s_p - mx).astype(BF16)
    else:
        mx = jnp.max(s_c, axis=2, keepdims=True)
    p_c = jnp.exp(s_c - mx).astype(BF16)
    res = []
    for hh in range(2):
        rows = slice(hh * blk, (hh + 1) * blk)
        if kp is not None:
            res.append(_bdot(jnp.concatenate([p_c[:, rows], p_p[:, rows]], axis=2),
                             jnp.concatenate([voc[hh], vop[hh]], axis=1)))
        else:
            res.append(_bdot(p_c[:, rows], voc[hh]))
    hi = lax.broadcasted_iota(jnp.int32, q.shape, 2) >= ATT_HD
    return (jnp.where(hi, res[1], res[0]), jnp.where(hi, res[0], res[1]),
            jnp.where(hi, mx[:, blk:], mx[:, :blk]))


def _attn_kernel(*refs, dils, seq, gb):
    n_g = len(dils)
    qkv = refs[:3 * n_g]
    o_ref = refs[3 * n_g]
    num_ref, den_ref, max_ref = refs[3 * n_g + 1:3 * n_g + 4]
    blk = ATT_BLOCK
    span = gb * blk

    for g, dil in enumerate(dils):
        q_ref, k_ref, v_ref = qkv[3 * g:3 * g + 3]
        nb = seq // dil // blk
        assert nb % gb == 0 or gb % nb == 0
        runs = max(1, gb // nb)
        per = gb // runs

        def body(c, carry, q_ref=q_ref, k_ref=k_ref, v_ref=v_ref, nb=nb, dil=dil, g=g, runs=runs, per=per):
            r0 = (c * gb) // nb
            n0 = (c * gb) % nb
            if runs == 1:
                cur = pl.ds(pl.multiple_of(n0 * blk, blk), span)
                load = lambda ref: ref[r0, cur, :].reshape(gb, blk, LANES)
            else:
                load = lambda ref: ref[pl.ds(r0, runs)].reshape(gb, blk, LANES)
            q, kc = load(q_ref), load(k_ref)
            voc = _value_operands(load(v_ref))
            if nb == 1:
                num, den, mx = _attn_blocks(q, kc, voc, None, None, 0, 1)
            else:
                if runs == 1:
                    first = pl.ds(pl.multiple_of(jnp.maximum(n0 - 1, 0) * blk, blk), blk)
                    kf = k_ref[r0, first, :][None]
                    vof = _value_operands(v_ref[r0, first, :][None])
                else:
                    kf, vof = kc[:1], [voc[0][:1], voc[1][:1]]
                kp = jnp.concatenate([kf, kc[:gb - 1]], axis=0)
                vop = [jnp.concatenate([vof[hh], voc[hh][:gb - 1]], axis=0) for hh in range(2)]
                num, den, mx = _attn_blocks(q, kc, voc, kp, vop, n0, nb)
            for t in range(runs):
                rows = slice(t * per, (t + 1) * per)
                start = (r0 + t) + dil * n0 * blk
                if dil == 1:
                    dst = pl.ds(pl.multiple_of(start, blk), per * blk)
                else:
                    dst = pl.ds(start, per * blk, stride=dil)
                num_ref[g, dst, :] = num[rows].reshape(per * blk, LANES)
                den_ref[g, dst, :] = den[rows].reshape(per * blk, LANES)
                max_ref[g, dst, :] = mx[rows].reshape(per * blk, LANES)
            return carry

        lax.fori_loop(0, seq // span, body, 0, unroll=True)

    rows = 2 * blk

    def mix(t, carry):
        sl = pl.ds(pl.multiple_of(t * rows, rows), rows)
        ms = [max_ref[g, sl, :] for g in range(n_g)]
        top = functools.reduce(jnp.maximum, ms)
        ws = [jnp.exp(m - top) for m in ms]
        total = functools.reduce(jnp.add, [ws[g] * pltpu.roll(den_ref[g, sl, :], ATT_HD, axis=1)
                                           for g in range(n_g)])
        inv = 1.0 / total
        acc = functools.reduce(jnp.add, [(ws[g] * inv) * num_ref[g, sl, :] for g in range(n_g)])
        o_ref[sl, :] = acc.astype(BF16)
        return carry

    lax.fori_loop(0, seq // rows, mix, 0)


def _attention(qkvs, *, batch, seq, dils, width, gb=8):
    n_pairs = width // LANES
    args, specs = [], []
    for arr, dil in zip(qkvs, dils):
        length = seq // dil
        for part in range(3):
            args.append(arr)
            specs.append(pl.BlockSpec((dil, length, LANES),
                                      lambda b, hp, part=part: (0, b, part * n_pairs + hp)))
    n_g = len(dils)
    vmem = (2 * 3 * n_g * seq * LANES * 2 + 3 * n_g * seq * LANES * 4 + 2 * seq * LANES * 2
            + 32 * gb * ATT_BLOCK * LANES * 4 + (8 << 20))
    return pl.pallas_call(
        functools.partial(_attn_kernel, dils=tuple(dils), seq=seq, gb=gb),
        grid=(batch, n_pairs),
        in_specs=specs,
        out_specs=pl.BlockSpec((None, seq, LANES), lambda b, hp: (b, 0, hp)),
        out_shape=jax.ShapeDtypeStruct((batch, seq, width), BF16),
        scratch_shapes=[pltpu.VMEM((n_g, seq, LANES), F32)] * 3,
        compiler_params=_params(("parallel", "parallel"), vmem),
        name="dilated_attention",
    )(*args)


def _log_sigmoid(z):
    return jnp.minimum(z, 0.0) - jnp.log1p(jnp.exp(-jnp.abs(z)))


def _mlstm_proj_kernel(x_ref, g_ref, wq_ref, wkt_ref, wv_ref, wo_ref, wgc_ref, wgr_ref, bc_ref, br_ref,
                       q_ref, kt_ref, v_ref, og_ref, gc_ref, gr_ref):
    xn = _rms(x_ref[...], g_ref[...]).astype(BF16)
    q_ref[...] = _dot(xn, wq_ref[...]).astype(BF16)
    kt_ref[...] = (_dot_nt(wkt_ref[...], xn) * (ML_DQK ** -0.5)).astype(BF16)
    v_ref[...] = _dot(xn, wv_ref[...]).astype(BF16)
    og_ref[...] = _dot(xn, wo_ref[...])
    zc = _dot(xn, wgc_ref[...]) + bc_ref[...]
    lane = lax.broadcasted_iota(jnp.int32, zc.shape, 1)
    gc_ref[...] = jnp.where(lane >= ML_HEADS, _log_sigmoid(zc), zc)
    zr = _dot_nt(wgr_ref[...], xn) + br_ref[...]
    rowi = lax.broadcasted_iota(jnp.int32, zr.shape, 0)
    gr_ref[...] = jnp.where(rowi >= ML_HEADS, _log_sigmoid(zr), zr)


def _mlstm_proj(x, g_pre, w_in, b_i, b_f, *, tm=512):
    m, d = x.shape
    nh = ML_HEADS
    o1 = nh * ML_DQK
    o2 = 2 * o1
    o3 = o2 + nh * ML_DV
    o4 = o3 + nh * ML_DV
    wq = w_in[:, :o1].astype(BF16)
    wkt = w_in[:, o1:o2].T.astype(BF16)
    wv = w_in[:, o2:o3].astype(BF16)
    wo = w_in[:, o3:o4].astype(BF16)
    wg = w_in[:, o4:]
    wgc = jnp.pad(wg, ((0, 0), (0, LANES - 2 * nh))).astype(BF16)
    wgr = wg.T.astype(BF16)
    bias = jnp.concatenate([b_i, b_f]).astype(F32)
    bc = jnp.pad(bias, (0, LANES - 2 * nh))[None, :]
    br = bias[:, None]
    tm = min(tm, m)
    assert m % tm == 0 and 2 * nh == SUBLANES
    dv = nh * ML_DV
    vmem = (2 * tm * d * 4 + 2 * d * (2 * o1 + 2 * dv + LANES + SUBLANES) * 2
            + 2 * tm * (2 * o1 * 2 + dv * 2 + dv * 4 + LANES * 4 + SUBLANES * 4)
            + tm * (2 * o1 + 2 * dv) * 4 + tm * d * 6 + (4 << 20))
    full = lambda shape: pl.BlockSpec(shape, lambda i: (0, 0))
    return pl.pallas_call(
        _mlstm_proj_kernel,
        grid=(m // tm,),
        in_specs=[
            pl.BlockSpec((tm, d), lambda i: (i, 0)),
            full((1, d)), full((d, o1)), full((o1, d)), full((d, dv)), full((d, dv)),
            full((d, LANES)), full((2 * nh, d)), full((1, LANES)), full((2 * nh, 1)),
        ],
        out_specs=[
            pl.BlockSpec((tm, o1), lambda i: (i, 0)),
            pl.BlockSpec((o1, tm), lambda i: (0, i)),
            pl.BlockSpec((tm, dv), lambda i: (i, 0)),
            pl.BlockSpec((tm, dv), lambda i: (i, 0)),
            pl.BlockSpec((tm, LANES), lambda i: (i, 0)),
            pl.BlockSpec((2 * nh, tm), lambda i: (0, i)),
        ],
        out_shape=[
            jax.ShapeDtypeStruct((m, o1), BF16),
            jax.ShapeDtypeStruct((o1, m), BF16),
            jax.ShapeDtypeStruct((m, dv), BF16),
            jax.ShapeDtypeStruct((m, dv), F32),
            jax.ShapeDtypeStruct((m, LANES), F32),
            jax.ShapeDtypeStruct((2 * nh, m), F32),
        ],
        compiler_params=_params(("parallel",), vmem),
        name="mlstm_proj",
    )(x, g_pre, wq, wkt, wv, wo, wgc, wgr, bc, br)


def _mlstm_kernel(q_ref, kt_ref, v_ref, og_ref, gc_ref, gr_ref, o_ref, st_ref, *, seq):
    nh, lc, dqk, dv = ML_HEADS, ML_CHUNK, ML_DQK, ML_DV
    st_ref[...] = jnp.zeros(st_ref.shape, F32)
    row = lax.broadcasted_iota(jnp.int32, (lc, lc), 0)
    col = lax.broadcasted_iota(jnp.int32, (lc, lc), 1)
    causal = col <= row
    tril = causal.astype(F32)
    triu = (row <= col).astype(F32)
    lane0 = lax.broadcasted_iota(jnp.int32, (lc, LANES), 1) == 0

    def body(j, ms):
        pos = pl.ds(pl.multiple_of(j * lc, lc), lc)
        gr_c = gr_ref[:, pos]
        gc_c = gc_ref[pos, :]
        b_rows = _dot_f32(gr_c, triu)
        b_cols = _dot_f32(tril, gc_c)
        new_ms = []
        for h in range(nh):
            m_h = ms[h]
            g_row = gr_c[h:h + 1, :] - b_rows[nh + h:nh + h + 1, :]
            b_col = b_cols[:, nh + h:nh + h + 1]
            g_col = gc_c[:, h:h + 1] - b_col
            dmat = jnp.where(causal, g_row, NEG_INF)
            mu = jnp.maximum(m_h, jnp.max(dmat, axis=1, keepdims=True))
            w = jnp.exp(dmat - mu)
            wi = jnp.exp(m_h - mu)
            emt = jnp.exp(-(b_col + mu))
            qh = q_ref[pos, h * dqk:(h + 1) * dqk]
            kth = kt_ref[h * dqk:(h + 1) * dqk, pos]
            vh = v_ref[pos, h * dv:(h + 1) * dv]
            a = _dot(qh, kth) * w
            state = st_ref[h]
            qc = _dot(qh, state.astype(BF16))
            num = _dot(a.astype(BF16), vh) + wi * qc[:, :dv]
            den = jnp.sum(a, axis=1, keepdims=True) + wi * qc[:, dv:dv + 1]
            hval = num / jnp.maximum(jnp.abs(den), emt)
            gate = jax.nn.sigmoid(og_ref[pos, h * dv:(h + 1) * dv])
            o_ref[pos, h * dv:(h + 1) * dv] = (hval * gate).astype(BF16)
            mu_last = mu[lc - 1:lc, :]
            we = jnp.exp(g_col - mu_last)
            decay = jnp.exp(m_h - mu_last)
            vaug = jnp.concatenate([vh.astype(F32) * we, jnp.where(lane0, we, 0.0)], axis=1)
            st_ref[h] = decay * state + _dot(kth, vaug.astype(BF16))
            new_ms.append(b_col[lc - 1:lc, :] + mu_last)
        return tuple(new_ms)

    lax.fori_loop(0, seq // lc, body, tuple(jnp.zeros((1, 1), F32) for _ in range(nh)))


def _mlstm_scan(q, kt, v, og, gc, gr, *, batch, seq):
    nh, dqk, dv = ML_HEADS, ML_DQK, ML_DV
    assert seq % ML_CHUNK == 0
    vmem = (2 * seq * (2 * nh * dqk * 2 + nh * dv * 2 + nh * dv * 4 + LANES * 4 + SUBLANES * 4 + nh * dv * 2)
            + nh * dqk * (dv + LANES) * 4 + (8 << 20))
    return pl.pallas_call(
        functools.partial(_mlstm_kernel, seq=seq),
        grid=(batch,),
        in_specs=[
            pl.BlockSpec((seq, nh * dqk), lambda b: (b, 0)),
            pl.BlockSpec((nh * dqk, seq), lambda b: (0, b)),
            pl.BlockSpec((seq, nh * dv), lambda b: (b, 0)),
            pl.BlockSpec((seq, nh * dv), lambda b: (b, 0)),
            pl.BlockSpec((seq, LANES), lambda b: (b, 0)),
            pl.BlockSpec((2 * nh, seq), lambda b: (0, b)),
        ],
        out_specs=pl.BlockSpec((seq, nh * dv), lambda b: (b, 0)),
        out_shape=jax.ShapeDtypeStruct((batch * seq, nh * dv), BF16),
        scratch_shapes=[pltpu.VMEM((nh, dqk, dv + LANES), F32)],
        compiler_params=_params(("parallel",), vmem),
        name="mlstm_scan",
    )(q, kt, v, og, gc, gr)


def kernel(x, norm_g, ffn1_wg, ffn1_wu, ffn1_wd, ffn2_wg, ffn2_wu, ffn2_wd,
           conv_w_in, conv_k, conv_w_out, attn_w_in, attn_w_out,
           mlstm_w_in, mlstm_b_i, mlstm_b_f, mlstm_w_out):
    batch, seq, d = x.shape
    depth = norm_g.shape[0]
    h = x.reshape(batch * seq, d)
    dils = tuple(dil for _, dil in DILATION_PAIRS)
    width = ATT_HEADS * ATT_HD
    for i in range(depth):
        g = [norm_g[i, k][None, :] for k in range(norm_g.shape[1])]
        h = _ffn(h, g[0], g[1], ffn1_wg, ffn1_wu, ffn1_wd, layer=i)
        mixer, j = i % N_MIXERS, i // N_MIXERS
        mix = None
        if mixer == 0:
            h = _conv_mixer(h, g[2], g[3], conv_w_in[j].astype(BF16), conv_k[j],
                            conv_w_out[j].astype(BF16), seq=seq)
        elif mixer == 1:
            w_in = attn_w_in[j].astype(BF16)
            qkvs = [_qkv_proj(h, g[2], w_in, group=gi, width=width, batch=batch, seq=seq, dil=dil)
                    for gi, dil in enumerate(dils)]
            att = _attention(qkvs, batch=batch, seq=seq, dils=dils, width=width)
            mix = (att.reshape(batch * seq, width), attn_w_out, j, g[3])
        else:
            q, kt, v, og, gc, gr = _mlstm_proj(h, g[2], mlstm_w_in[j], mlstm_b_i[j], mlstm_b_f[j])
            hg = _mlstm_scan(q, kt, v, og, gc, gr, batch=batch, seq=seq)
            mix = (hg, mlstm_w_out, j, g[3])
        h = _ffn(h, g[4], g[5], ffn2_wg, ffn2_wu, ffn2_wd, layer=i, mix=mix)
    return h.reshape(batch, seq, d)
```

```python
import functools
import math

import jax
import jax.numpy as jnp
from jax import lax
from jax.experimental import pallas as pl
from jax.experimental.pallas import tpu as pltpu

LANES = 128
SUBLANES = 8
V7X_VMEM_BYTES = 64 * 1024 * 1024

EPS = 1e-6
CONV_W = 3
ATT_HEADS = 16
ATT_HD = 64
DILATION_PAIRS = ((128, 1), (512, 4), (2048, 16))
ATT_BLOCK = 128
ROPE_THETA = 500000.0
ROT_DIM = ATT_HD // 4
ML_HEADS = 4
ML_DQK = 128
ML_DV = 256
ML_CHUNK = 128
N_MIXERS = 3

BF16 = jnp.bfloat16
F32 = jnp.float32
NEG_INF = float("-inf")


def _params(semantics, vmem_bytes):
    return pltpu.CompilerParams(dimension_semantics=semantics,
                                vmem_limit_bytes=min(int(vmem_bytes), V7X_VMEM_BYTES - (4 << 20)))


def _rms(x32, g_row):
    ms = jnp.mean(x32 * x32, axis=-1, keepdims=True)
    return (x32 * lax.rsqrt(ms + EPS)) * g_row


def _dot(a, b):
    return jnp.dot(a, b, preferred_element_type=F32)


def _dot_nt(a, b):
    return lax.dot_general(a, b, (((1,), (1,)), ((), ())), preferred_element_type=F32)


def _dot_f32(a, b):
    return jnp.dot(a, b, preferred_element_type=F32, precision=lax.Precision.HIGHEST)


def _fill_bf16(jobs):
    chunks = []
    used = {}
    for src, dst, stage, sems in jobs:
        rc = stage.shape[1]
        for c in range(src.shape[0] // rc):
            slot = used.get(id(stage), 0) % 2
            used[id(stage)] = used.get(id(stage), 0) + 1
            rows = slice(c * rc, (c + 1) * rc)
            copy = pltpu.make_async_copy(src.at[rows, :], stage.at[slot], sems.at[slot])
            chunks.append((copy, dst, rows, stage, slot))
    chunks[0][0].start()
    for n, (copy, dst, rows, stage, slot) in enumerate(chunks):
        if n + 1 < len(chunks):
            chunks[n + 1][0].start()
        copy.wait()
        dst[rows, :] = stage[slot].astype(BF16)


def _ffn_kernel(*refs, cw, layer, mix_layer):
    if mix_layer is None:
        (x_ref, gpre_ref, gpost_ref, wg_hbm, wu_hbm, wd_hbm, o_ref,
         wg_ref, wu_ref, wd_ref, acc_ref, stage_in, stage_out, sem_in, sem_out) = refs
        jobs = []
    else:
        (x_ref, gpre_ref, gpost_ref, wg_hbm, wu_hbm, wd_hbm, a_ref, wmix_hbm, gmix_ref, o_ref,
         wg_ref, wu_ref, wd_ref, acc_ref, stage_in, stage_out, sem_in, sem_out,
         wmix_ref, stage_mix, sem_mix) = refs
        jobs = [(wmix_hbm.at[mix_layer], wmix_ref, stage_mix, sem_mix)]

    @pl.when(pl.program_id(0) == 0)
    def _():
        _fill_bf16(jobs + [(wg_hbm.at[layer], wg_ref, stage_in, sem_in),
                           (wu_hbm.at[layer], wu_ref, stage_in, sem_in),
                           (wd_hbm.at[layer], wd_ref, stage_out, sem_out)])

    if mix_layer is None:
        x_in = x_ref
    else:
        o_ref[...] = x_ref[...] + _rms(_dot(a_ref[...], wmix_ref[...]), gmix_ref[...])
        x_in = o_ref
    xn = _rms(x_in[...], gpre_ref[...]).astype(BF16)
    ff = wg_ref.shape[1]
    for c in range(ff // cw):
        lo = c * cw
        g = _dot(xn, wg_ref[:, lo:lo + cw])
        u = _dot(xn, wu_ref[:, lo:lo + cw])
        h = ((g * jax.nn.sigmoid(g)) * u).astype(BF16)
        part = _dot(h, wd_ref[lo:lo + cw, :])
        if c == 0:
            acc_ref[...] = part
        else:
            acc_ref[...] += part
    o_ref[...] = x_in[...] + 0.5 * _rms(acc_ref[...], gpost_ref[...])


def _resident(shape):
    return pl.BlockSpec(shape, lambda *_: (0,) * len(shape), pipeline_mode=pl.Buffered(1))


def _ffn(x, g_pre, g_post, wg, wu, wd, *, layer, mix=None, tm=1024, cw=256, stage_rows=128):
    m, d = x.shape
    ff = wg.shape[2]
    tm = min(tm, m)
    out_rows = stage_rows * ff // d
    assert m % tm == 0 and ff % cw == 0 and d % stage_rows == 0 and ff % out_rows == 0
    vmem = (4 * tm * d * 4
            + 3 * d * ff * 2
            + 4 * stage_rows * ff * 4
            + tm * d * (2 + 4 + 4)
            + 2 * tm * cw * (4 + 4 + 2 + 4)
            + (4 << 20))
    hbm = pl.BlockSpec(memory_space=pl.ANY)
    args = [x, g_pre, g_post, wg, wu, wd]
    in_specs = [pl.BlockSpec((tm, d), lambda i: (i, 0)), _resident((1, d)), _resident((1, d)), hbm, hbm, hbm]
    scratch = [
        pltpu.VMEM((d, ff), BF16), pltpu.VMEM((d, ff), BF16), pltpu.VMEM((ff, d), BF16),
        pltpu.VMEM((tm, d), F32),
        pltpu.VMEM((2, stage_rows, ff), F32), pltpu.VMEM((2, out_rows, d), F32),
        pltpu.SemaphoreType.DMA((2,)), pltpu.SemaphoreType.DMA((2,)),
    ]
    mix_layer = None
    if mix is not None:
        a, w_mix, mix_layer, g_mix = mix
        k = a.shape[1]
        assert k % stage_rows == 0
        args += [a, w_mix, g_mix]
        in_specs += [pl.BlockSpec((tm, k), lambda i: (i, 0)), hbm, _resident((1, d))]
        scratch += [pltpu.VMEM((k, d), BF16), pltpu.VMEM((2, stage_rows, d), F32), pltpu.SemaphoreType.DMA((2,))]
        vmem += 2 * tm * k * 2 + k * d * 2 + 2 * stage_rows * d * 4 + tm * d * 4
    return pl.pallas_call(
        functools.partial(_ffn_kernel, cw=cw, layer=layer, mix_layer=mix_layer),
        grid=(m // tm,),
        in_specs=in_specs,
        out_specs=pl.BlockSpec((tm, d), lambda i: (i, 0)),
        out_shape=jax.ShapeDtypeStruct((m, d), F32),
        scratch_shapes=scratch,
        compiler_params=_params(("arbitrary",), vmem),
        name="ffn" if mix is None else "outproj_ffn",
    )(*args)


def _conv_kernel(x_ref, gpre_ref, gpost_ref, win_ref, k_ref, wout_ref, o_ref, ext_ref, *, tiles_per_seq, nsub):
    i = pl.program_id(0)
    tm, d = x_ref.shape
    halo = SUBLANES
    sub = tm // nsub

    @pl.when(i % tiles_per_seq == 0)
    def _():
        ext_ref[0:halo, :] = jnp.zeros((halo, d), F32)

    kk = k_ref[...]
    for s in range(nsub):
        lo = s * sub
        xs = x_ref[lo:lo + sub, :]
        xn = _rms(xs, gpre_ref[...]).astype(BF16)
        p = _dot(xn, win_ref[...])
        cu = p[:, d:2 * d] * p[:, 2 * d:3 * d]
        ext_ref[halo + lo:halo + lo + sub, :] = cu
        conv = (kk[0:1, :] * cu
                + kk[1:2, :] * ext_ref[halo - 1 + lo:halo - 1 + lo + sub, :]
                + kk[2:3, :] * ext_ref[halo - 2 + lo:halo - 2 + lo + sub, :])
        y = _dot((p[:, 0:d] * conv).astype(BF16), wout_ref[...])
        o_ref[lo:lo + sub, :] = xs + _rms(y, gpost_ref[...])
    ext_ref[0:halo, :] = ext_ref[tm:tm + halo, :]


def _conv_mixer(x, g_pre, g_post, w_in, k, w_out, *, seq, tm=1024, nsub=2):
    m, d = x.shape
    tm = min(tm, seq)
    assert seq % tm == 0 and m % seq == 0 and tm % nsub == 0
    sub = tm // nsub
    vmem = (4 * tm * d * 4 + (3 * d * d + d * d) * 2 + (tm + SUBLANES) * d * 4
            + nsub * (sub * 3 * d * 4 + 6 * sub * d * 4) + (4 << 20))
    return pl.pallas_call(
        functools.partial(_conv_kernel, tiles_per_seq=seq // tm, nsub=nsub),
        grid=(m // tm,),
        in_specs=[
            pl.BlockSpec((tm, d), lambda i: (i, 0)),
            _resident((1, d)),
            _resident((1, d)),
            _resident((d, 3 * d)),
            _resident((CONV_W, d)),
            _resident((d, d)),
        ],
        out_specs=pl.BlockSpec((tm, d), lambda i: (i, 0)),
        out_shape=jax.ShapeDtypeStruct((m, d), F32),
        scratch_shapes=[pltpu.VMEM((tm + SUBLANES, d), F32)],
        compiler_params=_params(("arbitrary",), vmem),
        name="conv_mixer",
    )(x, g_pre, g_post, w_in, k, w_out)


def _rope_tables(seq, dil, rows):
    length = seq // dil
    u = jnp.arange(rows, dtype=jnp.int32) % length
    pos = (u[None, :] * dil + jnp.arange(dil, dtype=jnp.int32)[:, None]).astype(F32)
    inv = ROPE_THETA ** (-jnp.arange(0, ROT_DIM, 2, dtype=F32) / ROT_DIM)
    ang = pos[:, :, None] * inv[None, None, :]
    cos, sin = jnp.cos(ang), jnp.sin(ang)
    half = ROT_DIM // 2
    dd = jnp.arange(LANES) % ATT_HD
    fi = dd % half
    cos_l = jnp.take(cos, fi, axis=-1)
    sin_l = jnp.take(sin, fi, axis=-1)
    c_tab = jnp.where(dd < ROT_DIM, cos_l, 1.0)
    s_lo = jnp.where(dd < half, -sin_l, 0.0)
    s_hi = jnp.where((dd >= half) & (dd < ROT_DIM), sin_l, 0.0)
    return c_tab.astype(F32), s_lo.astype(F32), s_hi.astype(F32)


def _qkv_kernel(x_ref, g_ref, w_ref, c_ref, slo_ref, shi_ref, o_ref, slab_ref, xp_ref, *, width, dil):
    tm, d = x_ref.shape
    per = tm // dil
    xn = _rms(x_ref[...], g_ref[...])
    if dil == 1:
        xp = xn.astype(BF16)
    else:
        for cb in range(d // LANES):
            slab_ref[cb] = xn[:, cb * LANES:(cb + 1) * LANES]
        for r in range(dil):
            for cb in range(d // LANES):
                piece = slab_ref[cb, pl.ds(r, per, stride=dil), :]
                xp_ref[r * per:(r + 1) * per, cb * LANES:(cb + 1) * LANES] = piece.astype(BF16)
        xp = xp_ref[...]
    p = _dot(xp, w_ref[...])
    c_tab = c_ref[...].reshape(tm, LANES)
    s_lo = slo_ref[...].reshape(tm, LANES)
    s_hi = shi_ref[...].reshape(tm, LANES)
    half = ROT_DIM // 2
    scale = ATT_HD ** -0.5
    for part in range(2):
        for cb in range(width // LANES):
            lo = part * width + cb * LANES
            blk = p[:, lo:lo + LANES]
            rot = (blk * c_tab
                   + pltpu.roll(blk, LANES - half, axis=1) * s_lo
                   + pltpu.roll(blk, half, axis=1) * s_hi)
            if part == 0:
                rot = rot * scale
            o_ref[:, :, lo:lo + LANES] = rot.astype(BF16).reshape(dil, per, LANES)
    o_ref[:, :, 2 * width:3 * width] = p[:, 2 * width:3 * width].astype(BF16).reshape(dil, per, width)


def _qkv_proj(x, g_pre, w, *, group, width, batch, seq, dil, tm=512):
    m, d = x.shape
    tm = min(tm, seq)
    per = tm // dil
    assert seq % tm == 0 and tm % dil == 0 and per % (2 * SUBLANES) == 0
    c_tab, s_lo, s_hi = _rope_tables(seq, dil, seq // dil)
    tiles_per_seq = seq // tm
    tab_spec = pl.BlockSpec((dil, per, LANES), lambda i: (0, i % tiles_per_seq, 0))
    vmem = (2 * tm * d * 4 + d * 3 * width * 2 + 2 * tm * 3 * width * 2 + 6 * tm * LANES * 4
            + tm * 3 * width * 4 + tm * d * (4 + 4 + 2) + 4 * tm * width * 4 + (4 << 20))
    return pl.pallas_call(
        functools.partial(_qkv_kernel, width=width, dil=dil),
        grid=(m // tm,),
        in_specs=[
            pl.BlockSpec((tm, d), lambda i: (i, 0)),
            _resident((1, d)),
            pl.BlockSpec((d, 3 * width), lambda i: (0, group), pipeline_mode=pl.Buffered(1)),
            tab_spec, tab_spec, tab_spec,
        ],
        out_specs=pl.BlockSpec((dil, per, 3 * width), lambda i: (0, i, 0)),
        out_shape=jax.ShapeDtypeStruct((dil, m // dil, 3 * width), BF16),
        scratch_shapes=[pltpu.VMEM((d // LANES, tm, LANES), F32), pltpu.VMEM((tm, d), BF16)],
        compiler_params=_params(("parallel",), vmem),
        name=f"qkv_proj_d{dil}",
    )(x, g_pre, w, c_tab, s_lo, s_hi)


def _bdot_nt(a, b):
    return lax.dot_general(a, b, (((2,), (2,)), ((0,), (0,))), preferred_element_type=F32)


def _bdot(a, b):
    return lax.dot_general(a, b, (((2,), (1,)), ((0,), (0,))), preferred_element_type=F32)


def _value_operands(v):
    hi = lax.broadcasted_iota(jnp.int32, v.shape, 2).astype(F32).astype(v.dtype) >= ATT_HD
    one = jnp.ones_like(v)
    return jnp.where(hi, one, v), jnp.where(hi, v, one)


def _attn_blocks(q, kc, voc, kp, vop, phase, period):
    g, blk, _ = q.shape
    shape2 = (g, 2 * blk, LANES)
    row2 = lax.broadcasted_iota(jnp.int32, shape2, 1)
    lane = lax.broadcasted_iota(jnp.int32, shape2, 2)
    row = jnp.bitwise_and(row2, blk - 1)
    q2 = jnp.concatenate([q, q], axis=1)
    qs = jnp.where((row2 >= blk) == (lane >= ATT_HD), q2, jnp.zeros_like(q2))
    s_c = jnp.where(lane <= row, _bdot_nt(qs, kc), NEG_INF)
    if kp is not None:
        gidx = lax.broadcasted_iota(jnp.int32, shape2, 0)
        keep_prev = jnp.logical_and(lane >= row, jnp.bitwise_and(gidx + phase, period - 1) != 0)
        s_p = jnp.where(keep_prev, _bdot_nt(qs, kp), NEG_INF)
        mx = jnp.max(jnp.maximum(s_c, s_p), axis=2, keepdims=True)
        p_p = jnp.exp(s_p - mx).astype(BF16)
    else:
        mx = jnp.max(s_c, axis=2, keepdims=True)
    p_c = jnp.exp(s_c - mx).astype(BF16)
    res = []
    for hh in range(2):
        rows = slice(hh * blk, (hh + 1) * blk)
        if kp is not None:
            res.append(_bdot(jnp.concatenate([p_c[:, rows], p_p[:, rows]], axis=2),
                             jnp.concatenate([voc[hh], vop[hh]], axis=1)))
        else:
            res.append(_bdot(p_c[:, rows], voc[hh]))
    hi = lax.broadcasted_iota(jnp.int32, q.shape, 2) >= ATT_HD
    return (jnp.where(hi, res[1], res[0]), jnp.where(hi, res[0], res[1]),
            jnp.where(hi, mx[:, blk:], mx[:, :blk]))


def _attn_kernel(*refs, dils, seq, gb):
    n_g = len(dils)
    qkv = refs[:3 * n_g]
    o_ref = refs[3 * n_g]
    num_ref, den_ref, max_ref = refs[3 * n_g + 1:3 * n_g + 4]
    blk = ATT_BLOCK
    span = gb * blk
    order = sorted(range(n_g), key=lambda g: -dils[g])
    assert dils[order[-1]] == 1
    plane = {g: k for k, g in enumerate(order[:-1])}

    def mix(dst, num, den, mx):
        nums = [num] + [num_ref[k, dst, :] for k in range(n_g - 1)]
        dens = [den] + [den_ref[k, dst, :] for k in range(n_g - 1)]
        ms = [mx] + [max_ref[k, dst, :] for k in range(n_g - 1)]
        top = functools.reduce(jnp.maximum, ms)
        ws = [jnp.exp(m - top) for m in ms]
        total = functools.reduce(jnp.add, [w * pltpu.roll(d, ATT_HD, axis=1) for w, d in zip(ws, dens)])
        inv = 1.0 / total
        acc = functools.reduce(jnp.add, [(w * inv) * n for w, n in zip(ws, nums)])
        o_ref[dst, :] = acc.astype(BF16)

    for g in order:
        dil = dils[g]
        q_ref, k_ref, v_ref = qkv[3 * g:3 * g + 3]
        nb = seq // dil // blk
        assert nb % gb == 0 or gb % nb == 0
        runs = max(1, gb // nb)
        per = gb // runs

        def body(c, carry, q_ref=q_ref, k_ref=k_ref, v_ref=v_ref, nb=nb, dil=dil, g=g, runs=runs, per=per):
            r0 = (c * gb) // nb
            n0 = (c * gb) % nb
            if runs == 1:
                cur = pl.ds(pl.multiple_of(n0 * blk, blk), span)
                load = lambda ref: ref[r0, cur, :].reshape(gb, blk, LANES)
            else:
                load = lambda ref: ref[pl.ds(r0, runs)].reshape(gb, blk, LANES)
            q, kc = load(q_ref), load(k_ref)
            voc = _value_operands(load(v_ref))
            if nb == 1:
                num, den, mx = _attn_blocks(q, kc, voc, None, None, 0, 1)
            else:
                if runs == 1:
                    first = pl.ds(pl.multiple_of(jnp.maximum(n0 - 1, 0) * blk, blk), blk)
                    kf = k_ref[r0, first, :][None]
                    vof = _value_operands(v_ref[r0, first, :][None])
                else:
                    kf, vof = kc[:1], [voc[0][:1], voc[1][:1]]
                kp = jnp.concatenate([kf, kc[:gb - 1]], axis=0)
                vop = [jnp.concatenate([vof[hh], voc[hh][:gb - 1]], axis=0) for hh in range(2)]
                num, den, mx = _attn_blocks(q, kc, voc, kp, vop, n0, nb)
            if g == order[-1]:
                for t in range(gb):
                    mix(pl.ds(pl.multiple_of((n0 + t) * blk, blk), blk), num[t], den[t], mx[t])
                return carry
            for t in range(runs):
                rows = slice(t * per, (t + 1) * per)
                dst = pl.ds((r0 + t) + dil * n0 * blk, per * blk, stride=dil)
                num_ref[plane[g], dst, :] = num[rows].reshape(per * blk, LANES)
                den_ref[plane[g], dst, :] = den[rows].reshape(per * blk, LANES)
                max_ref[plane[g], dst, :] = mx[rows].reshape(per * blk, LANES)
            return carry

        lax.fori_loop(0, seq // span, body, 0, unroll=True)


def _attention(qkvs, *, batch, seq, dils, width, gb=8):
    n_pairs = width // LANES
    args, specs = [], []
    for arr, dil in zip(qkvs, dils):
        length = seq // dil
        for part in range(3):
            args.append(arr)
            specs.append(pl.BlockSpec((dil, length, LANES),
                                      lambda b, hp, part=part: (0, b, part * n_pairs + hp)))
    n_g = len(dils)
    vmem = (2 * 3 * n_g * seq * LANES * 2 + 3 * n_g * seq * LANES * 4 + 2 * seq * LANES * 2
            + 32 * gb * ATT_BLOCK * LANES * 4 + (8 << 20))
    return pl.pallas_call(
        functools.partial(_attn_kernel, dils=tuple(dils), seq=seq, gb=gb),
        grid=(batch, n_pairs),
        in_specs=specs,
        out_specs=pl.BlockSpec((None, seq, LANES), lambda b, hp: (b, 0, hp)),
        out_shape=jax.ShapeDtypeStruct((batch, seq, width), BF16),
        scratch_shapes=[pltpu.VMEM((n_g - 1, seq, LANES), F32)] * 3,
        compiler_params=_params(("parallel", "parallel"), vmem),
        name="dilated_attention",
    )(*args)


def _log_sigmoid(z):
    return jnp.minimum(z, 0.0) - jnp.log1p(jnp.exp(-jnp.abs(z)))


def _mlstm_proj_kernel(x_ref, g_ref, wq_ref, wkt_ref, wv_ref, wo_ref, wgc_ref, wgr_ref, bc_ref, br_ref,
                       q_ref, kt_ref, v_ref, og_ref, gc_ref, gr_ref):
    xn = _rms(x_ref[...], g_ref[...]).astype(BF16)
    q_ref[...] = _dot(xn, wq_ref[...]).astype(BF16)
    kt_ref[...] = (_dot_nt(wkt_ref[...], xn) * (ML_DQK ** -0.5)).astype(BF16)
    v_ref[...] = _dot(xn, wv_ref[...]).astype(BF16)
    og_ref[...] = _dot(xn, wo_ref[...])
    zc = _dot(xn, wgc_ref[...]) + bc_ref[...]
    lane = lax.broadcasted_iota(jnp.int32, zc.shape, 1)
    gc_ref[...] = jnp.where(lane >= ML_HEADS, _log_sigmoid(zc), zc)
    zr = _dot_nt(wgr_ref[...], xn) + br_ref[...]
    rowi = lax.broadcasted_iota(jnp.int32, zr.shape, 0)
    gr_ref[...] = jnp.where(rowi >= ML_HEADS, _log_sigmoid(zr), zr)


def _mlstm_proj(x, g_pre, w_in, b_i, b_f, *, tm=512):
    m, d = x.shape
    nh = ML_HEADS
    o1 = nh * ML_DQK
    o2 = 2 * o1
    o3 = o2 + nh * ML_DV
    o4 = o3 + nh * ML_DV
    wq = w_in[:, :o1].astype(BF16)
    wkt = w_in[:, o1:o2].T.astype(BF16)
    wv = w_in[:, o2:o3].astype(BF16)
    wo = w_in[:, o3:o4].astype(BF16)
    wg = w_in[:, o4:]
    wgc = jnp.pad(wg, ((0, 0), (0, LANES - 2 * nh))).astype(BF16)
    wgr = wg.T.astype(BF16)
    bias = jnp.concatenate([b_i, b_f]).astype(F32)
    bc = jnp.pad(bias, (0, LANES - 2 * nh))[None, :]
    br = bias[:, None]
    tm = min(tm, m)
    assert m % tm == 0 and 2 * nh == SUBLANES
    dv = nh * ML_DV
    vmem = (2 * tm * d * 4 + 2 * d * (2 * o1 + 2 * dv + LANES + SUBLANES) * 2
            + 2 * tm * (2 * o1 * 2 + dv * 2 + dv * 4 + LANES * 4 + SUBLANES * 4)
            + tm * (2 * o1 + 2 * dv) * 4 + tm * d * 6 + (4 << 20))
    full = lambda shape: pl.BlockSpec(shape, lambda i: (0, 0))
    return pl.pallas_call(
        _mlstm_proj_kernel,
        grid=(m // tm,),
        in_specs=[
            pl.BlockSpec((tm, d), lambda i: (i, 0)),
            full((1, d)), full((d, o1)), full((o1, d)), full((d, dv)), full((d, dv)),
            full((d, LANES)), full((2 * nh, d)), full((1, LANES)), full((2 * nh, 1)),
        ],
        out_specs=[
            pl.BlockSpec((tm, o1), lambda i: (i, 0)),
            pl.BlockSpec((o1, tm), lambda i: (0, i)),
            pl.BlockSpec((tm, dv), lambda i: (i, 0)),
            pl.BlockSpec((tm, dv), lambda i: (i, 0)),
            pl.BlockSpec((tm, LANES), lambda i: (i, 0)),
            pl.BlockSpec((2 * nh, tm), lambda i: (0, i)),
        ],
        out_shape=[
            jax.ShapeDtypeStruct((m, o1), BF16),
            jax.ShapeDtypeStruct((o1, m), BF16),
            jax.ShapeDtypeStruct((m, dv), BF16),
            jax.ShapeDtypeStruct((m, dv), F32),
            jax.ShapeDtypeStruct((m, LANES), F32),
            jax.ShapeDtypeStruct((2 * nh, m), F32),
        ],
        compiler_params=_params(("parallel",), vmem),
        name="mlstm_proj",
    )(x, g_pre, wq, wkt, wv, wo, wgc, wgr, bc, br)


def _mlstm_kernel(q_ref, kt_ref, v_ref, og_ref, gc_ref, gr_ref, o_ref, st_ref, *, seq):
    nh, lc, dqk, dv = ML_HEADS, ML_CHUNK, ML_DQK, ML_DV
    st_ref[...] = jnp.zeros(st_ref.shape, F32)
    row = lax.broadcasted_iota(jnp.int32, (lc, lc), 0)
    col = lax.broadcasted_iota(jnp.int32, (lc, lc), 1)
    causal = col <= row
    tril = causal.astype(F32)
    triu = (row <= col).astype(F32)
    lane0 = lax.broadcasted_iota(jnp.int32, (lc, LANES), 1) == 0

    def body(j, ms):
        pos = pl.ds(pl.multiple_of(j * lc, lc), lc)
        gr_c = gr_ref[:, pos]
        gc_c = gc_ref[pos, :]
        b_rows = _dot_f32(gr_c, triu)
        b_cols = _dot_f32(tril, gc_c)
        new_ms = []
        for h in range(nh):
            m_h = ms[h]
            g_row = gr_c[h:h + 1, :] - b_rows[nh + h:nh + h + 1, :]
            b_col = b_cols[:, nh + h:nh + h + 1]
            g_col = gc_c[:, h:h + 1] - b_col
            dmat = jnp.where(causal, g_row, NEG_INF)
            mu = jnp.maximum(m_h, jnp.max(dmat, axis=1, keepdims=True))
            w = jnp.exp(dmat - mu)
            wi = jnp.exp(m_h - mu)
            emt = jnp.exp(-(b_col + mu))
            qh = q_ref[pos, h * dqk:(h + 1) * dqk]
            kth = kt_ref[h * dqk:(h + 1) * dqk, pos]
            vh = v_ref[pos, h * dv:(h + 1) * dv]
            a = _dot(qh, kth) * w
            state = st_ref[h]
            qc = _dot(qh, state.astype(BF16))
            num = _dot(a.astype(BF16), vh) + wi * qc[:, :dv]
            den = jnp.sum(a, axis=1, keepdims=True) + wi * qc[:, dv:dv + 1]
            hval = num / jnp.maximum(jnp.abs(den), emt)
            gate = jax.nn.sigmoid(og_ref[pos, h * dv:(h + 1) * dv])
            o_ref[pos, h * dv:(h + 1) * dv] = (hval * gate).astype(BF16)
            mu_last = mu[lc - 1:lc, :]
            we = jnp.exp(g_col - mu_last)
            decay = jnp.exp(m_h - mu_last)
            vaug = jnp.concatenate([vh.astype(F32) * we, jnp.where(lane0, we, 0.0)], axis=1)
            st_ref[h] = decay * state + _dot(kth, vaug.astype(BF16))
            new_ms.append(b_col[lc - 1:lc, :] + mu_last)
        return tuple(new_ms)

    lax.fori_loop(0, seq // lc, body, tuple(jnp.zeros((1, 1), F32) for _ in range(nh)))


def _mlstm_scan(q, kt, v, og, gc, gr, *, batch, seq):
    nh, dqk, dv = ML_HEADS, ML_DQK, ML_DV
    assert seq % ML_CHUNK == 0
    vmem = (2 * seq * (2 * nh * dqk * 2 + nh * dv * 2 + nh * dv * 4 + LANES * 4 + SUBLANES * 4 + nh * dv * 2)
            + nh * dqk * (dv + LANES) * 4 + (8 << 20))
    return pl.pallas_call(
        functools.partial(_mlstm_kernel, seq=seq),
        grid=(batch,),
        in_specs=[
            pl.BlockSpec((seq, nh * dqk), lambda b: (b, 0)),
            pl.BlockSpec((nh * dqk, seq), lambda b: (0, b)),
            pl.BlockSpec((seq, nh * dv), lambda b: (b, 0)),
            pl.BlockSpec((seq, nh * dv), lambda b: (b, 0)),
            pl.BlockSpec((seq, LANES), lambda b: (b, 0)),
            pl.BlockSpec((2 * nh, seq), lambda b: (0, b)),
        ],
        out_specs=pl.BlockSpec((seq, nh * dv), lambda b: (b, 0)),
        out_shape=jax.ShapeDtypeStruct((batch * seq, nh * dv), BF16),
        scratch_shapes=[pltpu.VMEM((nh, dqk, dv + LANES), F32)],
        compiler_params=_params(("parallel",), vmem),
        name="mlstm_scan",
    )(q, kt, v, og, gc, gr)


def kernel(x, norm_g, ffn1_wg, ffn1_wu, ffn1_wd, ffn2_wg, ffn2_wu, ffn2_wd,
           conv_w_in, conv_k, conv_w_out, attn_w_in, attn_w_out,
           mlstm_w_in, mlstm_b_i, mlstm_b_f, mlstm_w_out):
    batch, seq, d = x.shape
    depth = norm_g.shape[0]
    h = x.reshape(batch * seq, d)
    dils = tuple(dil for _, dil in DILATION_PAIRS)
    width = ATT_HEADS * ATT_HD
    for i in range(depth):
        g = [norm_g[i, k][None, :] for k in range(norm_g.shape[1])]
        h = _ffn(h, g[0], g[1], ffn1_wg, ffn1_wu, ffn1_wd, layer=i)
        mixer, j = i % N_MIXERS, i // N_MIXERS
        mix = None
        if mixer == 0:
            h = _conv_mixer(h, g[2], g[3], conv_w_in[j].astype(BF16), conv_k[j],
                            conv_w_out[j].astype(BF16), seq=seq)
        elif mixer == 1:
            w_in = attn_w_in[j].astype(BF16)
            qkvs = [_qkv_proj(h, g[2], w_in, group=gi, width=width, batch=batch, seq=seq, dil=dil)
                    for gi, dil in enumerate(dils)]
            att = _attention(qkvs, batch=batch, seq=seq, dils=dils, width=width)
            mix = (att.reshape(batch * seq, width), attn_w_out, j, g[3])
        else:
            q, kt, v, og, gc, gr = _mlstm_proj(h, g[2], mlstm_w_in[j], mlstm_b_i[j], mlstm_b_f[j])
            hg = _mlstm_scan(q, kt, v, og, gc, gr, batch=batch, seq=seq)
            mix = (hg, mlstm_w_out, j, g[3])
        h = _ffn(h, g[4], g[5], ffn2_wg, ffn2_wu, ffn2_wd, layer=i, mix=mix)
    return h.reshape(batch, seq, d)
```

```python
import functools
import math

import jax
import jax.numpy as jnp
from jax import lax
from jax.experimental import pallas as pl
from jax.experimental.pallas import tpu as pltpu

LANES = 128
SUBLANES = 8
V7X_VMEM_BYTES = 64 * 1024 * 1024

EPS = 1e-6
CONV_W = 3
ATT_HEADS = 16
ATT_HD = 64
DILATION_PAIRS = ((128, 1), (512, 4), (2048, 16))
ATT_BLOCK = 128
ROPE_THETA = 500000.0
ROT_DIM = ATT_HD // 4
ML_HEADS = 4
ML_DQK = 128
ML_DV = 256
ML_CHUNK = 128
N_MIXERS = 3

BF16 = jnp.bfloat16
F32 = jnp.float32
NEG_INF = float("-inf")


def _params(semantics, vmem_bytes):
    return pltpu.CompilerParams(dimension_semantics=semantics,
                                vmem_limit_bytes=min(int(vmem_bytes), V7X_VMEM_BYTES - (4 << 20)))


def _rms(x32, g_row):
    ms = jnp.mean(x32 * x32, axis=-1, keepdims=True)
    return (x32 * lax.rsqrt(ms + EPS)) * g_row


def _dot(a, b):
    return jnp.dot(a, b, preferred_element_type=F32)


def _dot_nt(a, b):
    return lax.dot_general(a, b, (((1,), (1,)), ((), ())), preferred_element_type=F32)


def _dot_f32(a, b):
    return jnp.dot(a, b, preferred_element_type=F32, precision=lax.Precision.HIGHEST)


def _fill_bf16(jobs):
    chunks = []
    used = {}
    for src, dst, stage, sems in jobs:
        rc = stage.shape[1]
        for c in range(src.shape[0] // rc):
            slot = used.get(id(stage), 0) % 2
            used[id(stage)] = used.get(id(stage), 0) + 1
            rows = slice(c * rc, (c + 1) * rc)
            copy = pltpu.make_async_copy(src.at[rows, :], stage.at[slot], sems.at[slot])
            chunks.append((copy, dst, rows, stage, slot))
    chunks[0][0].start()
    for n, (copy, dst, rows, stage, slot) in enumerate(chunks):
        if n + 1 < len(chunks):
            chunks[n + 1][0].start()
        copy.wait()
        dst[rows, :] = stage[slot].astype(BF16)


def _ffn_kernel(*refs, cw, layer, mix_layer):
    if mix_layer is None:
        (x_ref, gpre_ref, gpost_ref, wg_hbm, wu_hbm, wd_hbm, o_ref,
         wg_ref, wu_ref, wd_ref, acc_ref, stage_g, stage_u, stage_d, sem_g, sem_u, sem_d) = refs
    else:
        (x_ref, gpre_ref, gpost_ref, wg_hbm, wu_hbm, wd_hbm, a_ref, wmix_hbm, gmix_ref, o_ref,
         wg_ref, wu_ref, wd_ref, acc_ref, stage_g, stage_u, stage_d, sem_g, sem_u, sem_d,
         wmix_ref, stage_mix, sem_mix) = refs
    ff = wg_ref.shape[1]
    n_chunks = ff // cw

    def copies(c):
        slot, cols = c % 2, slice(c * cw, (c + 1) * cw)
        return [pltpu.make_async_copy(wg_hbm.at[layer, :, cols], stage_g.at[slot], sem_g.at[slot]),
                pltpu.make_async_copy(wu_hbm.at[layer, :, cols], stage_u.at[slot], sem_u.at[slot]),
                pltpu.make_async_copy(wd_hbm.at[layer, cols, :], stage_d.at[slot], sem_d.at[slot])]

    def arrive(c):
        if c + 1 < n_chunks:
            for copy in copies(c + 1):
                copy.start()
        for copy in copies(c):
            copy.wait()
        slot, cols = c % 2, slice(c * cw, (c + 1) * cw)
        wg_ref[:, cols] = stage_g[slot].astype(BF16)
        wu_ref[:, cols] = stage_u[slot].astype(BF16)
        wd_ref[cols, :] = stage_d[slot].astype(BF16)

    def tile(first):
        if first:
            for copy in copies(0):
                copy.start()
            if mix_layer is not None:
                _fill_bf16([(wmix_hbm.at[mix_layer], wmix_ref, stage_mix, sem_mix)])
        if mix_layer is None:
            x_in = x_ref
        else:
            o_ref[...] = x_ref[...] + _rms(_dot(a_ref[...], wmix_ref[...]), gmix_ref[...])
            x_in = o_ref
        xn = _rms(x_in[...], gpre_ref[...]).astype(BF16)
        for c in range(n_chunks):
            if first:
                arrive(c)
            lo = c * cw
            g = _dot(xn, wg_ref[:, lo:lo + cw])
            u = _dot(xn, wu_ref[:, lo:lo + cw])
            h = ((g * jax.nn.sigmoid(g)) * u).astype(BF16)
            part = _dot(h, wd_ref[lo:lo + cw, :])
            if c == 0:
                acc_ref[...] = part
            else:
                acc_ref[...] += part
        o_ref[...] = x_in[...] + 0.5 * _rms(acc_ref[...], gpost_ref[...])

    pl.when(pl.program_id(0) == 0)(functools.partial(tile, True))
    pl.when(pl.program_id(0) > 0)(functools.partial(tile, False))


def _resident(shape):
    return pl.BlockSpec(shape, lambda *_: (0,) * len(shape), pipeline_mode=pl.Buffered(1))


def _ffn(x, g_pre, g_post, wg, wu, wd, *, layer, mix=None, tm=1024, cw=256, stage_rows=128):
    m, d = x.shape
    ff = wg.shape[2]
    tm = min(tm, m)
    assert m % tm == 0 and ff % cw == 0
    vmem = (4 * tm * d * 4
            + 3 * d * ff * 2
            + 3 * 2 * d * cw * 4
            + tm * d * (2 + 4 + 4)
            + 2 * tm * cw * (4 + 4 + 2 + 4)
            + (4 << 20))
    hbm = pl.BlockSpec(memory_space=pl.ANY)
    args = [x, g_pre, g_post, wg, wu, wd]
    in_specs = [pl.BlockSpec((tm, d), lambda i: (i, 0)), _resident((1, d)), _resident((1, d)), hbm, hbm, hbm]
    scratch = [
        pltpu.VMEM((d, ff), BF16), pltpu.VMEM((d, ff), BF16), pltpu.VMEM((ff, d), BF16),
        pltpu.VMEM((tm, d), F32),
        pltpu.VMEM((2, d, cw), F32), pltpu.VMEM((2, d, cw), F32), pltpu.VMEM((2, cw, d), F32),
        pltpu.SemaphoreType.DMA((2,)), pltpu.SemaphoreType.DMA((2,)), pltpu.SemaphoreType.DMA((2,)),
    ]
    mix_layer = None
    if mix is not None:
        a, w_mix, mix_layer, g_mix = mix
        k = a.shape[1]
        assert k % stage_rows == 0
        args += [a, w_mix, g_mix]
        in_specs += [pl.BlockSpec((tm, k), lambda i: (i, 0)), hbm, _resident((1, d))]
        scratch += [pltpu.VMEM((k, d), BF16), pltpu.VMEM((2, stage_rows, d), F32), pltpu.SemaphoreType.DMA((2,))]
        vmem += 2 * tm * k * 2 + k * d * 2 + 2 * stage_rows * d * 4 + tm * d * 4
    return pl.pallas_call(
        functools.partial(_ffn_kernel, cw=cw, layer=layer, mix_layer=mix_layer),
        grid=(m // tm,),
        in_specs=in_specs,
        out_specs=pl.BlockSpec((tm, d), lambda i: (i, 0)),
        out_shape=jax.ShapeDtypeStruct((m, d), F32),
        scratch_shapes=scratch,
        compiler_params=_params(("arbitrary",), vmem),
        name="ffn" if mix is None else "outproj_ffn",
    )(*args)


def _conv_kernel(x_ref, gpre_ref, gpost_ref, win_ref, k_ref, wout_ref, o_ref, ext_ref, *, tiles_per_seq, nsub):
    i = pl.program_id(0)
    tm, d = x_ref.shape
    halo = SUBLANES
    sub = tm // nsub

    @pl.when(i % tiles_per_seq == 0)
    def _():
        ext_ref[0:halo, :] = jnp.zeros((halo, d), F32)

    kk = k_ref[...]
    for s in range(nsub):
        lo = s * sub
        xs = x_ref[lo:lo + sub, :]
        xn = _rms(xs, gpre_ref[...]).astype(BF16)
        p = _dot(xn, win_ref[...])
        cu = p[:, d:2 * d] * p[:, 2 * d:3 * d]
        ext_ref[halo + lo:halo + lo + sub, :] = cu
        conv = (kk[0:1, :] * cu
                + kk[1:2, :] * ext_ref[halo - 1 + lo:halo - 1 + lo + sub, :]
                + kk[2:3, :] * ext_ref[halo - 2 + lo:halo - 2 + lo + sub, :])
        y = _dot((p[:, 0:d] * conv).astype(BF16), wout_ref[...])
        o_ref[lo:lo + sub, :] = xs + _rms(y, gpost_ref[...])
    ext_ref[0:halo, :] = ext_ref[tm:tm + halo, :]


def _conv_mixer(x, g_pre, g_post, w_in, k, w_out, *, seq, tm=1024, nsub=2):
    m, d = x.shape
    tm = min(tm, seq)
    assert seq % tm == 0 and m % seq == 0 and tm % nsub == 0
    sub = tm // nsub
    vmem = (4 * tm * d * 4 + (3 * d * d + d * d) * 2 + (tm + SUBLANES) * d * 4
            + nsub * (sub * 3 * d * 4 + 6 * sub * d * 4) + (4 << 20))
    return pl.pallas_call(
        functools.partial(_conv_kernel, tiles_per_seq=seq // tm, nsub=nsub),
        grid=(m // tm,),
        in_specs=[
            pl.BlockSpec((tm, d), lambda i: (i, 0)),
            _resident((1, d)),
            _resident((1, d)),
            _resident((d, 3 * d)),
            _resident((CONV_W, d)),
            _resident((d, d)),
        ],
        out_specs=pl.BlockSpec((tm, d), lambda i: (i, 0)),
        out_shape=jax.ShapeDtypeStruct((m, d), F32),
        scratch_shapes=[pltpu.VMEM((tm + SUBLANES, d), F32)],
        compiler_params=_params(("arbitrary",), vmem),
        name="conv_mixer",
    )(x, g_pre, g_post, w_in, k, w_out)


def _rope_tables(seq, dil, rows):
    length = seq // dil
    u = jnp.arange(rows, dtype=jnp.int32) % length
    pos = (u[None, :] * dil + jnp.arange(dil, dtype=jnp.int32)[:, None]).astype(F32)
    inv = ROPE_THETA ** (-jnp.arange(0, ROT_DIM, 2, dtype=F32) / ROT_DIM)
    ang = pos[:, :, None] * inv[None, None, :]
    cos, sin = jnp.cos(ang), jnp.sin(ang)
    half = ROT_DIM // 2
    dd = jnp.arange(LANES) % ATT_HD
    fi = dd % half
    cos_l = jnp.take(cos, fi, axis=-1)
    sin_l = jnp.take(sin, fi, axis=-1)
    c_tab = jnp.where(dd < ROT_DIM, cos_l, 1.0)
    s_lo = jnp.where(dd < half, -sin_l, 0.0)
    s_hi = jnp.where((dd >= half) & (dd < ROT_DIM), sin_l, 0.0)
    return c_tab.astype(F32), s_lo.astype(F32), s_hi.astype(F32)


def _qkv_kernel(x_ref, g_ref, w_ref, c_ref, slo_ref, shi_ref, o_ref, slab_ref, xp_ref, *, width, dil):
    tm, d = x_ref.shape
    per = tm // dil
    xn = _rms(x_ref[...], g_ref[...])
    if dil == 1:
        xp = xn.astype(BF16)
    else:
        for cb in range(d // LANES):
            slab_ref[cb] = xn[:, cb * LANES:(cb + 1) * LANES]
        for r in range(dil):
            for cb in range(d // LANES):
                piece = slab_ref[cb, pl.ds(r, per, stride=dil), :]
                xp_ref[r * per:(r + 1) * per, cb * LANES:(cb + 1) * LANES] = piece.astype(BF16)
        xp = xp_ref[...]
    p = _dot(xp, w_ref[...])
    c_tab = c_ref[...].reshape(tm, LANES)
    s_lo = slo_ref[...].reshape(tm, LANES)
    s_hi = shi_ref[...].reshape(tm, LANES)
    half = ROT_DIM // 2
    scale = ATT_HD ** -0.5
    for part in range(2):
        for cb in range(width // LANES):
            lo = part * width + cb * LANES
            blk = p[:, lo:lo + LANES]
            rot = (blk * c_tab
                   + pltpu.roll(blk, LANES - half, axis=1) * s_lo
                   + pltpu.roll(blk, half, axis=1) * s_hi)
            if part == 0:
                rot = rot * scale
            o_ref[:, :, lo:lo + LANES] = rot.astype(BF16).reshape(dil, per, LANES)
    o_ref[:, :, 2 * width:3 * width] = p[:, 2 * width:3 * width].astype(BF16).reshape(dil, per, width)


def _qkv_proj(x, g_pre, w, *, group, width, batch, seq, dil, tm=1024):
    m, d = x.shape
    tm = min(tm, seq)
    per = tm // dil
    assert seq % tm == 0 and tm % dil == 0 and per % (2 * SUBLANES) == 0
    c_tab, s_lo, s_hi = _rope_tables(seq, dil, seq // dil)
    tiles_per_seq = seq // tm
    tab_spec = pl.BlockSpec((dil, per, LANES), lambda i: (0, i % tiles_per_seq, 0))
    vmem = (2 * tm * d * 4 + d * 3 * width * 2 + 2 * tm * 3 * width * 2 + 6 * tm * LANES * 4
            + tm * 3 * width * 4 + tm * d * (4 + 4 + 2) + 4 * tm * width * 4 + (4 << 20))
    return pl.pallas_call(
        functools.partial(_qkv_kernel, width=width, dil=dil),
        grid=(m // tm,),
        in_specs=[
            pl.BlockSpec((tm, d), lambda i: (i, 0)),
            _resident((1, d)),
            pl.BlockSpec((d, 3 * width), lambda i: (0, group), pipeline_mode=pl.Buffered(1)),
            tab_spec, tab_spec, tab_spec,
        ],
        out_specs=pl.BlockSpec((dil, per, 3 * width), lambda i: (0, i, 0)),
        out_shape=jax.ShapeDtypeStruct((dil, m // dil, 3 * width), BF16),
        scratch_shapes=[pltpu.VMEM((d // LANES, tm, LANES), F32), pltpu.VMEM((tm, d), BF16)],
        compiler_params=_params(("parallel",), vmem),
        name=f"qkv_proj_d{dil}",
    )(x, g_pre, w, c_tab, s_lo, s_hi)


def _bdot_nt(a, b):
    return lax.dot_general(a, b, (((2,), (2,)), ((0,), (0,))), preferred_element_type=F32)


def _bdot(a, b):
    return lax.dot_general(a, b, (((2,), (1,)), ((0,), (0,))), preferred_element_type=F32)


def _value_operands(v):
    hi = lax.broadcasted_iota(jnp.int32, v.shape, 2).astype(F32).astype(v.dtype) >= ATT_HD
    one = jnp.ones_like(v)
    return jnp.where(hi, one, v), jnp.where(hi, v, one)


def _attn_blocks(q, kc, voc, kp, vop, phase, period):
    g, blk, _ = q.shape
    shape2 = (g, 2 * blk, LANES)
    row2 = lax.broadcasted_iota(jnp.int32, shape2, 1)
    lane = lax.broadcasted_iota(jnp.int32, shape2, 2)
    row = jnp.bitwise_and(row2, blk - 1)
    q2 = jnp.concatenate([q, q], axis=1)
    qs = jnp.where((row2 >= blk) == (lane >= ATT_HD), q2, jnp.zeros_like(q2))
    s_c = jnp.where(lane <= row, _bdot_nt(qs, kc), NEG_INF)
    if kp is not None:
        gidx = lax.broadcasted_iota(jnp.int32, shape2, 0)
        keep_prev = jnp.logical_and(lane >= row, jnp.bitwise_and(gidx + phase, period - 1) != 0)
        s_p = jnp.where(keep_prev, _bdot_nt(qs, kp), NEG_INF)
        mx = jnp.max(jnp.maximum(s_c, s_p), axis=2, keepdims=True)
        p_p = jnp.exp(s_p - mx).astype(BF16)
    else:
        mx = jnp.max(s_c, axis=2, keepdims=True)
    p_c = jnp.exp(s_c - mx).astype(BF16)
    res = []
    for hh in range(2):
        rows = slice(hh * blk, (hh + 1) * blk)
        if kp is not None:
            res.append(_bdot(jnp.concatenate([p_c[:, rows], p_p[:, rows]], axis=2),
                             jnp.concatenate([voc[hh], vop[hh]], axis=1)))
        else:
            res.append(_bdot(p_c[:, rows], voc[hh]))
    hi = lax.broadcasted_iota(jnp.int32, q.shape, 2) >= ATT_HD
    return (jnp.where(hi, res[1], res[0]), jnp.where(hi, res[0], res[1]),
            jnp.where(hi, mx[:, blk:], mx[:, :blk]))


def _attn_kernel(*refs, dils, seq, gb):
    n_g = len(dils)
    qkv = refs[:3 * n_g]
    o_ref = refs[3 * n_g]
    num_ref, den_ref, max_ref = refs[3 * n_g + 1:3 * n_g + 4]
    blk = ATT_BLOCK
    span = gb * blk
    order = sorted(range(n_g), key=lambda g: -dils[g])
    assert dils[order[-1]] == 1
    plane = {g: k for k, g in enumerate(order[:-1])}

    def mix(dst, num, den, mx):
        nums = [num] + [num_ref[k, dst, :] for k in range(n_g - 1)]
        dens = [den] + [den_ref[k, dst, :] for k in range(n_g - 1)]
        ms = [mx] + [max_ref[k, dst, :] for k in range(n_g - 1)]
        top = functools.reduce(jnp.maximum, ms)
        ws = [jnp.exp(m - top) for m in ms]
        total = functools.reduce(jnp.add, [w * pltpu.roll(d, ATT_HD, axis=1) for w, d in zip(ws, dens)])
        inv = 1.0 / total
        acc = functools.reduce(jnp.add, [(w * inv) * n for w, n in zip(ws, nums)])
        o_ref[dst, :] = acc.astype(BF16)

    for g in order:
        dil = dils[g]
        q_ref, k_ref, v_ref = qkv[3 * g:3 * g + 3]
        nb = seq // dil // blk
        assert nb % gb == 0 or gb % nb == 0
        runs = max(1, gb // nb)
        per = gb // runs

        def body(c, carry, q_ref=q_ref, k_ref=k_ref, v_ref=v_ref, nb=nb, dil=dil, g=g, runs=runs, per=per):
            r0 = (c * gb) // nb
            n0 = (c * gb) % nb
            if runs == 1:
                cur = pl.ds(pl.multiple_of(n0 * blk, blk), span)
                load = lambda ref: ref[r0, cur, :].reshape(gb, blk, LANES)
            else:
                load = lambda ref: ref[pl.ds(r0, runs)].reshape(gb, blk, LANES)
            q, kc = load(q_ref), load(k_ref)
            voc = _value_operands(load(v_ref))
            if nb == 1:
                num, den, mx = _attn_blocks(q, kc, voc, None, None, 0, 1)
            else:
                if runs == 1:
                    first = pl.ds(pl.multiple_of(jnp.maximum(n0 - 1, 0) * blk, blk), blk)
                    kf = k_ref[r0, first, :][None]
                    vof = _value_operands(v_ref[r0, first, :][None])
                else:
                    kf, vof = kc[:1], [voc[0][:1], voc[1][:1]]
                kp = jnp.concatenate([kf, kc[:gb - 1]], axis=0)
                vop = [jnp.concatenate([vof[hh], voc[hh][:gb - 1]], axis=0) for hh in range(2)]
                num, den, mx = _attn_blocks(q, kc, voc, kp, vop, n0, nb)
            if g == order[-1]:
                for t in range(gb):
                    mix(pl.ds(pl.multiple_of((n0 + t) * blk, blk), blk), num[t], den[t], mx[t])
                return carry
            for t in range(runs):
                rows = slice(t * per, (t + 1) * per)
                dst = pl.ds((r0 + t) + dil * n0 * blk, per * blk, stride=dil)
                num_ref[plane[g], dst, :] = num[rows].reshape(per * blk, LANES)
                den_ref[plane[g], dst, :] = den[rows].reshape(per * blk, LANES)
                max_ref[plane[g], dst, :] = mx[rows].reshape(per * blk, LANES)
            return carry

        lax.fori_loop(0, seq // span, body, 0, unroll=True)


def _attention(qkvs, *, batch, seq, dils, width, gb=8):
    n_pairs = width // LANES
    args, specs = [], []
    for arr, dil in zip(qkvs, dils):
        length = seq // dil
        for part in range(3):
            args.append(arr)
            specs.append(pl.BlockSpec((dil, length, LANES),
                                      lambda b, hp, part=part: (0, b, part * n_pairs + hp)))
    n_g = len(dils)
    vmem = (2 * 3 * n_g * seq * LANES * 2 + 3 * n_g * seq * LANES * 4 + 2 * seq * LANES * 2
            + 32 * gb * ATT_BLOCK * LANES * 4 + (8 << 20))
    return pl.pallas_call(
        functools.partial(_attn_kernel, dils=tuple(dils), seq=seq, gb=gb),
        grid=(batch, n_pairs),
        in_specs=specs,
        out_specs=pl.BlockSpec((None, seq, LANES), lambda b, hp: (b, 0, hp)),
        out_shape=jax.ShapeDtypeStruct((batch, seq, width), BF16),
        scratch_shapes=[pltpu.VMEM((n_g - 1, seq, LANES), F32)] * 3,
        compiler_params=_params(("parallel", "parallel"), vmem),
        name="dilated_attention",
    )(*args)


def _log_sigmoid(z):
    return jnp.minimum(z, 0.0) - jnp.log1p(jnp.exp(-jnp.abs(z)))


def _mlstm_proj_kernel(x_ref, g_ref, wq_ref, wkt_ref, wv_ref, wo_ref, wgc_ref, wgr_ref, bc_ref, br_ref,
                       q_ref, kt_ref, v_ref, og_ref, gc_ref, gr_ref):
    xn = _rms(x_ref[...], g_ref[...]).astype(BF16)
    q_ref[...] = _dot(xn, wq_ref[...]).astype(BF16)
    kt_ref[...] = (_dot_nt(wkt_ref[...], xn) * (ML_DQK ** -0.5)).astype(BF16)
    v_ref[...] = _dot(xn, wv_ref[...]).astype(BF16)
    og_ref[...] = _dot(xn, wo_ref[...])
    zc = _dot(xn, wgc_ref[...]) + bc_ref[...]
    lane = lax.broadcasted_iota(jnp.int32, zc.shape, 1)
    gc_ref[...] = jnp.where(lane >= ML_HEADS, _log_sigmoid(zc), zc)
    zr = _dot_nt(wgr_ref[...], xn) + br_ref[...]
    rowi = lax.broadcasted_iota(jnp.int32, zr.shape, 0)
    gr_ref[...] = jnp.where(rowi >= ML_HEADS, _log_sigmoid(zr), zr)


def _mlstm_proj(x, g_pre, w_in, b_i, b_f, *, tm=1024):
    m, d = x.shape
    nh = ML_HEADS
    o1 = nh * ML_DQK
    o2 = 2 * o1
    o3 = o2 + nh * ML_DV
    o4 = o3 + nh * ML_DV
    wq = w_in[:, :o1].astype(BF16)
    wkt = w_in[:, o1:o2].T.astype(BF16)
    wv = w_in[:, o2:o3].astype(BF16)
    wo = w_in[:, o3:o4].astype(BF16)
    wg = w_in[:, o4:]
    wgc = jnp.pad(wg, ((0, 0), (0, LANES - 2 * nh))).astype(BF16)
    wgr = wg.T.astype(BF16)
    bias = jnp.concatenate([b_i, b_f]).astype(F32)
    bc = jnp.pad(bias, (0, LANES - 2 * nh))[None, :]
    br = bias[:, None]
    tm = min(tm, m)
    assert m % tm == 0 and 2 * nh == SUBLANES
    dv = nh * ML_DV
    vmem = (2 * tm * d * 4 + 2 * d * (2 * o1 + 2 * dv + LANES + SUBLANES) * 2
            + 2 * tm * (2 * o1 * 2 + dv * 2 + dv * 4 + LANES * 4 + SUBLANES * 4)
            + tm * (2 * o1 + 2 * dv) * 4 + tm * d * 6 + (4 << 20))
    full = lambda shape: pl.BlockSpec(shape, lambda i: (0, 0))
    return pl.pallas_call(
        _mlstm_proj_kernel,
        grid=(m // tm,),
        in_specs=[
            pl.BlockSpec((tm, d), lambda i: (i, 0)),
            full((1, d)), full((d, o1)), full((o1, d)), full((d, dv)), full((d, dv)),
            full((d, LANES)), full((2 * nh, d)), full((1, LANES)), full((2 * nh, 1)),
        ],
        out_specs=[
            pl.BlockSpec((tm, o1), lambda i: (i, 0)),
            pl.BlockSpec((o1, tm), lambda i: (0, i)),
            pl.BlockSpec((tm, dv), lambda i: (i, 0)),
            pl.BlockSpec((tm, dv), lambda i: (i, 0)),
            pl.BlockSpec((tm, LANES), lambda i: (i, 0)),
            pl.BlockSpec((2 * nh, tm), lambda i: (0, i)),
        ],
        out_shape=[
            jax.ShapeDtypeStruct((m, o1), BF16),
            jax.ShapeDtypeStruct((o1, m), BF16),
            jax.ShapeDtypeStruct((m, dv), BF16),
            jax.ShapeDtypeStruct((m, dv), F32),
            jax.ShapeDtypeStruct((m, LANES), F32),
            jax.ShapeDtypeStruct((2 * nh, m), F32),
        ],
        compiler_params=_params(("parallel",), vmem),
        name="mlstm_proj",
    )(x, g_pre, wq, wkt, wv, wo, wgc, wgr, bc, br)


def _mlstm_kernel(q_ref, kt_ref, v_ref, og_ref, gc_ref, gr_ref, o_ref, st_ref, *, seq):
    nh, lc, dqk, dv = ML_HEADS, ML_CHUNK, ML_DQK, ML_DV
    st_ref[...] = jnp.zeros(st_ref.shape, F32)
    row = lax.broadcasted_iota(jnp.int32, (lc, lc), 0)
    col = lax.broadcasted_iota(jnp.int32, (lc, lc), 1)
    causal = col <= row
    tril = causal.astype(F32)
    triu = (row <= col).astype(F32)
    lane0 = lax.broadcasted_iota(jnp.int32, (lc, LANES), 1) == 0

    def body(j, ms):
        pos = pl.ds(pl.multiple_of(j * lc, lc), lc)
        gr_c = gr_ref[:, pos]
        gc_c = gc_ref[pos, :]
        b_rows = _dot_f32(gr_c, triu)
        b_cols = _dot_f32(tril, gc_c)
        new_ms = []
        for h in range(nh):
            m_h = ms[h]
            g_row = gr_c[h:h + 1, :] - b_rows[nh + h:nh + h + 1, :]
            b_col = b_cols[:, nh + h:nh + h + 1]
            g_col = gc_c[:, h:h + 1] - b_col
            dmat = jnp.where(causal, g_row, NEG_INF)
            mu = jnp.maximum(m_h, jnp.max(dmat, axis=1, keepdims=True))
            w = jnp.exp(dmat - mu)
            wi = jnp.exp(m_h - mu)
            emt = jnp.exp(-(b_col + mu))
            qh = q_ref[pos, h * dqk:(h + 1) * dqk]
            kth = kt_ref[h * dqk:(h + 1) * dqk, pos]
            vh = v_ref[pos, h * dv:(h + 1) * dv]
            a = _dot(qh, kth) * w
            state = st_ref[h]
            qc = _dot(qh, state.astype(BF16))
            num = _dot(a.astype(BF16), vh) + wi * qc[:, :dv]
            den = jnp.sum(a, axis=1, keepdims=True) + wi * qc[:, dv:dv + 1]
            hval = num / jnp.maximum(jnp.abs(den), emt)
            gate = jax.nn.sigmoid(og_ref[pos, h * dv:(h + 1) * dv])
            o_ref[pos, h * dv:(h + 1) * dv] = (hval * gate).astype(BF16)
            mu_last = mu[lc - 1:lc, :]
            we = jnp.exp(g_col - mu_last)
            decay = jnp.exp(m_h - mu_last)
            vaug = jnp.concatenate([vh.astype(F32) * we, jnp.where(lane0, we, 0.0)], axis=1)
            st_ref[h] = decay * state + _dot(kth, vaug.astype(BF16))
            new_ms.append(b_col[lc - 1:lc, :] + mu_last)
        return tuple(new_ms)

    lax.fori_loop(0, seq // lc, body, tuple(jnp.zeros((1, 1), F32) for _ in range(nh)))


def _mlstm_scan(q, kt, v, og, gc, gr, *, batch, seq):
    nh, dqk, dv = ML_HEADS, ML_DQK, ML_DV
    assert seq % ML_CHUNK == 0
    vmem = (2 * seq * (2 * nh * dqk * 2 + nh * dv * 2 + nh * dv * 4 + LANES * 4 + SUBLANES * 4 + nh * dv * 2)
            + nh * dqk * (dv + LANES) * 4 + (8 << 20))
    return pl.pallas_call(
        functools.partial(_mlstm_kernel, seq=seq),
        grid=(batch,),
        in_specs=[
            pl.BlockSpec((seq, nh * dqk), lambda b: (b, 0)),
            pl.BlockSpec((nh * dqk, seq), lambda b: (0, b)),
            pl.BlockSpec((seq, nh * dv), lambda b: (b, 0)),
            pl.BlockSpec((seq, nh * dv), lambda b: (b, 0)),
            pl.BlockSpec((seq, LANES), lambda b: (b, 0)),
            pl.BlockSpec((2 * nh, seq), lambda b: (0, b)),
        ],
        out_specs=pl.BlockSpec((seq, nh * dv), lambda b: (b, 0)),
        out_shape=jax.ShapeDtypeStruct((batch * seq, nh * dv), BF16),
        scratch_shapes=[pltpu.VMEM((nh, dqk, dv + LANES), F32)],
        compiler_params=_params(("parallel",), vmem),
        name="mlstm_scan",
    )(q, kt, v, og, gc, gr)


def kernel(x, norm_g, ffn1_wg, ffn1_wu, ffn1_wd, ffn2_wg, ffn2_wu, ffn2_wd,
           conv_w_in, conv_k, conv_w_out, attn_w_in, attn_w_out,
           mlstm_w_in, mlstm_b_i, mlstm_b_f, mlstm_w_out):
    batch, seq, d = x.shape
    depth = norm_g.shape[0]
    h = x.reshape(batch * seq, d)
    dils = tuple(dil for _, dil in DILATION_PAIRS)
    width = ATT_HEADS * ATT_HD
    for i in range(depth):
        g = [norm_g[i, k][None, :] for k in range(norm_g.shape[1])]
        h = _ffn(h, g[0], g[1], ffn1_wg, ffn1_wu, ffn1_wd, layer=i)
        mixer, j = i % N_MIXERS, i // N_MIXERS
        mix = None
        if mixer == 0:
            h = _conv_mixer(h, g[2], g[3], conv_w_in[j].astype(BF16), conv_k[j],
                            conv_w_out[j].astype(BF16), seq=seq)
        elif mixer == 1:
            w_in = attn_w_in[j].astype(BF16)
            qkvs = [_qkv_proj(h, g[2], w_in, group=gi, width=width, batch=batch, seq=seq, dil=dil)
                    for gi, dil in enumerate(dils)]
            att = _attention(qkvs, batch=batch, seq=seq, dils=dils, width=width)
            mix = (att.reshape(batch * seq, width), attn_w_out, j, g[3])
        else:
            q, kt, v, og, gc, gr = _mlstm_proj(h, g[2], mlstm_w_in[j], mlstm_b_i[j], mlstm_b_f[j])
            hg = _mlstm_scan(q, kt, v, og, gc, gr, batch=batch, seq=seq)
            mix = (hg, mlstm_w_out, j, g[3])
        h = _ffn(h, g[4], g[5], ffn2_wg, ffn2_wu, ffn2_wd, layer=i, mix=mix)
    return h.reshape(batch, seq, d)
```

```python
import functools
import math

import jax
import jax.numpy as jnp
from jax import lax
from jax.experimental import pallas as pl
from jax.experimental.pallas import tpu as pltpu

LANES = 128
SUBLANES = 8
V7X_VMEM_BYTES = 64 * 1024 * 1024

EPS = 1e-6
CONV_W = 3
ATT_HEADS = 16
ATT_HD = 64
DILATION_PAIRS = ((128, 1), (512, 4), (2048, 16))
ATT_BLOCK = 128
ROPE_THETA = 500000.0
ROT_DIM = ATT_HD // 4
ML_HEADS = 4
ML_DQK = 128
ML_DV = 256
ML_CHUNK = 128
N_MIXERS = 3

BF16 = jnp.bfloat16
F32 = jnp.float32
NEG_INF = float("-inf")


def _params(semantics, vmem_bytes):
    return pltpu.CompilerParams(dimension_semantics=semantics,
                                vmem_limit_bytes=min(int(vmem_bytes), V7X_VMEM_BYTES - (4 << 20)))


def _rms(x32, g_row):
    ms = jnp.mean(x32 * x32, axis=-1, keepdims=True)
    return (x32 * lax.rsqrt(ms + EPS)) * g_row


def _dot(a, b):
    return jnp.dot(a, b, preferred_element_type=F32)


def _dot_nt(a, b):
    return lax.dot_general(a, b, (((1,), (1,)), ((), ())), preferred_element_type=F32)


def _dot_f32(a, b):
    return jnp.dot(a, b, preferred_element_type=F32, precision=lax.Precision.HIGHEST)


def _fill_bf16(jobs):
    chunks = []
    used = {}
    for src, dst, stage, sems in jobs:
        rc = stage.shape[1]
        for c in range(src.shape[0] // rc):
            slot = used.get(id(stage), 0) % 2
            used[id(stage)] = used.get(id(stage), 0) + 1
            rows = slice(c * rc, (c + 1) * rc)
            copy = pltpu.make_async_copy(src.at[rows, :], stage.at[slot], sems.at[slot])
            chunks.append((copy, dst, rows, stage, slot))
    chunks[0][0].start()
    for n, (copy, dst, rows, stage, slot) in enumerate(chunks):
        if n + 1 < len(chunks):
            chunks[n + 1][0].start()
        copy.wait()
        dst[rows, :] = stage[slot].astype(BF16)


def _ffn_kernel(*refs, cw, layer, mix_layer):
    if mix_layer is None:
        (x_ref, gpre_ref, gpost_ref, wg_hbm, wu_hbm, wd_hbm, o_ref,
         wg_ref, wu_ref, wd_ref, acc_ref, stage_g, stage_u, stage_d, sem_g, sem_u, sem_d) = refs
    else:
        (x_ref, gpre_ref, gpost_ref, wg_hbm, wu_hbm, wd_hbm, a_ref, wmix_hbm, gmix_ref, o_ref,
         wg_ref, wu_ref, wd_ref, acc_ref, stage_g, stage_u, stage_d, sem_g, sem_u, sem_d,
         wmix_ref, stage_mix, sem_mix) = refs
    ff = wg_ref.shape[1]
    n_chunks = ff // cw

    def copies(c):
        slot, cols = c % 2, pl.ds(pl.multiple_of(c * cw, cw), cw)
        return [pltpu.make_async_copy(wg_hbm.at[layer, :, cols], stage_g.at[slot], sem_g.at[slot]),
                pltpu.make_async_copy(wu_hbm.at[layer, :, cols], stage_u.at[slot], sem_u.at[slot]),
                pltpu.make_async_copy(wd_hbm.at[layer, cols, :], stage_d.at[slot], sem_d.at[slot])]

    def chunk(xn, cols):
        g = _dot(xn, wg_ref[:, cols])
        u = _dot(xn, wu_ref[:, cols])
        h = ((g * jax.nn.sigmoid(g)) * u).astype(BF16)
        return _dot(h, wd_ref[cols, :])

    def tile(first):
        if first:
            for copy in copies(0):
                copy.start()
            if mix_layer is not None:
                _fill_bf16([(wmix_hbm.at[mix_layer], wmix_ref, stage_mix, sem_mix)])
        if mix_layer is None:
            x_in = x_ref
        else:
            o_ref[...] = x_ref[...] + _rms(_dot(a_ref[...], wmix_ref[...]), gmix_ref[...])
            x_in = o_ref
        xn = _rms(x_in[...], gpre_ref[...]).astype(BF16)
        if first:
            acc_ref[...] = jnp.zeros(acc_ref.shape, F32)

            def body(c, carry):
                @pl.when(c + 1 < n_chunks)
                def _():
                    for copy in copies(c + 1):
                        copy.start()

                for copy in copies(c):
                    copy.wait()
                slot, cols = c % 2, pl.ds(pl.multiple_of(c * cw, cw), cw)
                wg_ref[:, cols] = stage_g[slot].astype(BF16)
                wu_ref[:, cols] = stage_u[slot].astype(BF16)
                wd_ref[cols, :] = stage_d[slot].astype(BF16)
                acc_ref[...] += chunk(xn, cols)
                return carry

            lax.fori_loop(0, n_chunks, body, 0)
        else:
            for c in range(n_chunks):
                part = chunk(xn, slice(c * cw, (c + 1) * cw))
                if c == 0:
                    acc_ref[...] = part
                else:
                    acc_ref[...] += part
        o_ref[...] = x_in[...] + 0.5 * _rms(acc_ref[...], gpost_ref[...])

    pl.when(pl.program_id(0) == 0)(functools.partial(tile, True))
    pl.when(pl.program_id(0) > 0)(functools.partial(tile, False))


def _resident(shape):
    return pl.BlockSpec(shape, lambda *_: (0,) * len(shape), pipeline_mode=pl.Buffered(1))


def _ffn(x, g_pre, g_post, wg, wu, wd, *, layer, mix=None, tm=1024, cw=256, stage_rows=128):
    m, d = x.shape
    ff = wg.shape[2]
    tm = min(tm, m)
    assert m % tm == 0 and ff % cw == 0
    vmem = (4 * tm * d * 4
            + 3 * d * ff * 2
            + 3 * 2 * d * cw * 4
            + tm * d * (2 + 4 + 4)
            + 2 * tm * cw * (4 + 4 + 2 + 4)
            + (4 << 20))
    hbm = pl.BlockSpec(memory_space=pl.ANY)
    args = [x, g_pre, g_post, wg, wu, wd]
    in_specs = [pl.BlockSpec((tm, d), lambda i: (i, 0)), _resident((1, d)), _resident((1, d)), hbm, hbm, hbm]
    scratch = [
        pltpu.VMEM((d, ff), BF16), pltpu.VMEM((d, ff), BF16), pltpu.VMEM((ff, d), BF16),
        pltpu.VMEM((tm, d), F32),
        pltpu.VMEM((2, d, cw), F32), pltpu.VMEM((2, d, cw), F32), pltpu.VMEM((2, cw, d), F32),
        pltpu.SemaphoreType.DMA((2,)), pltpu.SemaphoreType.DMA((2,)), pltpu.SemaphoreType.DMA((2,)),
    ]
    mix_layer = None
    if mix is not None:
        a, w_mix, mix_layer, g_mix = mix
        k = a.shape[1]
        assert k % stage_rows == 0
        args += [a, w_mix, g_mix]
        in_specs += [pl.BlockSpec((tm, k), lambda i: (i, 0)), hbm, _resident((1, d))]
        scratch += [pltpu.VMEM((k, d), BF16), pltpu.VMEM((2, stage_rows, d), F32), pltpu.SemaphoreType.DMA((2,))]
        vmem += 2 * tm * k * 2 + k * d * 2 + 2 * stage_rows * d * 4 + tm * d * 4
    return pl.pallas_call(
        functools.partial(_ffn_kernel, cw=cw, layer=layer, mix_layer=mix_layer),
        grid=(m // tm,),
        in_specs=in_specs,
        out_specs=pl.BlockSpec((tm, d), lambda i: (i, 0)),
        out_shape=jax.ShapeDtypeStruct((m, d), F32),
        scratch_shapes=scratch,
        compiler_params=_params(("arbitrary",), vmem),
        name="ffn" if mix is None else "outproj_ffn",
    )(*args)


def _conv_kernel(x_ref, gpre_ref, gpost_ref, win_ref, k_ref, wout_ref, o_ref, ext_ref, *, tiles_per_seq, nsub):
    i = pl.program_id(0)
    tm, d = x_ref.shape
    halo = SUBLANES
    sub = tm // nsub

    @pl.when(i % tiles_per_seq == 0)
    def _():
        ext_ref[0:halo, :] = jnp.zeros((halo, d), F32)

    kk = k_ref[...]
    for s in range(nsub):
        lo = s * sub
        xs = x_ref[lo:lo + sub, :]
        xn = _rms(xs, gpre_ref[...]).astype(BF16)
        p = _dot(xn, win_ref[...])
        cu = p[:, d:2 * d] * p[:, 2 * d:3 * d]
        ext_ref[halo + lo:halo + lo + sub, :] = cu
        conv = (kk[0:1, :] * cu
                + kk[1:2, :] * ext_ref[halo - 1 + lo:halo - 1 + lo + sub, :]
                + kk[2:3, :] * ext_ref[halo - 2 + lo:halo - 2 + lo + sub, :])
        y = _dot((p[:, 0:d] * conv).astype(BF16), wout_ref[...])
        o_ref[lo:lo + sub, :] = xs + _rms(y, gpost_ref[...])
    ext_ref[0:halo, :] = ext_ref[tm:tm + halo, :]


def _conv_mixer(x, g_pre, g_post, w_in, k, w_out, *, seq, tm=1024, nsub=2):
    m, d = x.shape
    tm = min(tm, seq)
    assert seq % tm == 0 and m % seq == 0 and tm % nsub == 0
    sub = tm // nsub
    vmem = (4 * tm * d * 4 + (3 * d * d + d * d) * 2 + (tm + SUBLANES) * d * 4
            + nsub * (sub * 3 * d * 4 + 6 * sub * d * 4) + (4 << 20))
    return pl.pallas_call(
        functools.partial(_conv_kernel, tiles_per_seq=seq // tm, nsub=nsub),
        grid=(m // tm,),
        in_specs=[
            pl.BlockSpec((tm, d), lambda i: (i, 0)),
            _resident((1, d)),
            _resident((1, d)),
            _resident((d, 3 * d)),
            _resident((CONV_W, d)),
            _resident((d, d)),
        ],
        out_specs=pl.BlockSpec((tm, d), lambda i: (i, 0)),
        out_shape=jax.ShapeDtypeStruct((m, d), F32),
        scratch_shapes=[pltpu.VMEM((tm + SUBLANES, d), F32)],
        compiler_params=_params(("arbitrary",), vmem),
        name="conv_mixer",
    )(x, g_pre, g_post, w_in, k, w_out)


def _rope_tables(seq, dil, rows):
    length = seq // dil
    u = jnp.arange(rows, dtype=jnp.int32) % length
    pos = (u[None, :] * dil + jnp.arange(dil, dtype=jnp.int32)[:, None]).astype(F32)
    inv = ROPE_THETA ** (-jnp.arange(0, ROT_DIM, 2, dtype=F32) / ROT_DIM)
    ang = pos[:, :, None] * inv[None, None, :]
    cos, sin = jnp.cos(ang), jnp.sin(ang)
    half = ROT_DIM // 2
    dd = jnp.arange(LANES) % ATT_HD
    fi = dd % half
    cos_l = jnp.take(cos, fi, axis=-1)
    sin_l = jnp.take(sin, fi, axis=-1)
    c_tab = jnp.where(dd < ROT_DIM, cos_l, 1.0)
    s_lo = jnp.where(dd < half, -sin_l, 0.0)
    s_hi = jnp.where((dd >= half) & (dd < ROT_DIM), sin_l, 0.0)
    return c_tab.astype(F32), s_lo.astype(F32), s_hi.astype(F32)


def _qkv_kernel(x_ref, g_ref, w_ref, c_ref, slo_ref, shi_ref, o_ref, slab_ref, xp_ref, *, width, dil):
    tm, d = x_ref.shape
    per = tm // dil
    xn = _rms(x_ref[...], g_ref[...])
    if dil == 1:
        xp = xn.astype(BF16)
    else:
        for cb in range(d // LANES):
            slab_ref[cb] = xn[:, cb * LANES:(cb + 1) * LANES]
        for r in range(dil):
            for cb in range(d // LANES):
                piece = slab_ref[cb, pl.ds(r, per, stride=dil), :]
                xp_ref[r * per:(r + 1) * per, cb * LANES:(cb + 1) * LANES] = piece.astype(BF16)
        xp = xp_ref[...]
    p = _dot(xp, w_ref[...])
    c_tab = c_ref[...].reshape(tm, LANES)
    s_lo = slo_ref[...].reshape(tm, LANES)
    s_hi = shi_ref[...].reshape(tm, LANES)
    half = ROT_DIM // 2
    scale = ATT_HD ** -0.5
    for part in range(2):
        for cb in range(width // LANES):
            lo = part * width + cb * LANES
            blk = p[:, lo:lo + LANES]
            rot = (blk * c_tab
                   + pltpu.roll(blk, LANES - half, axis=1) * s_lo
                   + pltpu.roll(blk, half, axis=1) * s_hi)
            if part == 0:
                rot = rot * scale
            o_ref[:, :, lo:lo + LANES] = rot.astype(BF16).reshape(dil, per, LANES)
    o_ref[:, :, 2 * width:3 * width] = p[:, 2 * width:3 * width].astype(BF16).reshape(dil, per, width)


def _qkv_proj(x, g_pre, w, *, group, width, batch, seq, dil, tm=1024):
    m, d = x.shape
    tm = min(tm, seq)
    per = tm // dil
    assert seq % tm == 0 and tm % dil == 0 and per % (2 * SUBLANES) == 0
    c_tab, s_lo, s_hi = _rope_tables(seq, dil, seq // dil)
    tiles_per_seq = seq // tm
    tab_spec = pl.BlockSpec((dil, per, LANES), lambda i: (0, i % tiles_per_seq, 0))
    vmem = (2 * tm * d * 4 + d * 3 * width * 2 + 2 * tm * 3 * width * 2 + 6 * tm * LANES * 4
            + tm * 3 * width * 4 + tm * d * (4 + 4 + 2) + 4 * tm * width * 4 + (4 << 20))
    return pl.pallas_call(
        functools.partial(_qkv_kernel, width=width, dil=dil),
        grid=(m // tm,),
        in_specs=[
            pl.BlockSpec((tm, d), lambda i: (i, 0)),
            _resident((1, d)),
            pl.BlockSpec((d, 3 * width), lambda i: (0, group), pipeline_mode=pl.Buffered(1)),
            tab_spec, tab_spec, tab_spec,
        ],
        out_specs=pl.BlockSpec((dil, per, 3 * width), lambda i: (0, i, 0)),
        out_shape=jax.ShapeDtypeStruct((dil, m // dil, 3 * width), BF16),
        scratch_shapes=[pltpu.VMEM((d // LANES, tm, LANES), F32), pltpu.VMEM((tm, d), BF16)],
        compiler_params=_params(("parallel",), vmem),
        name=f"qkv_proj_d{dil}",
    )(x, g_pre, w, c_tab, s_lo, s_hi)


def _bdot_nt(a, b):
    return lax.dot_general(a, b, (((2,), (2,)), ((0,), (0,))), preferred_element_type=F32)


def _bdot(a, b):
    return lax.dot_general(a, b, (((2,), (1,)), ((0,), (0,))), preferred_element_type=F32)


def _value_operands(v):
    hi = lax.broadcasted_iota(jnp.int32, v.shape, 2).astype(F32).astype(v.dtype) >= ATT_HD
    one = jnp.ones_like(v)
    return jnp.where(hi, one, v), jnp.where(hi, v, one)


def _attn_blocks(q, kc, voc, kp, vop, phase, period):
    g, blk, _ = q.shape
    shape2 = (g, 2 * blk, LANES)
    row2 = lax.broadcasted_iota(jnp.int32, shape2, 1)
    lane = lax.broadcasted_iota(jnp.int32, shape2, 2)
    row = jnp.bitwise_and(row2, blk - 1)
    q2 = jnp.concatenate([q, q], axis=1)
    qs = jnp.where((row2 >= blk) == (lane >= ATT_HD), q2, jnp.zeros_like(q2))
    s_c = jnp.where(lane <= row, _bdot_nt(qs, kc), NEG_INF)
    if kp is not None:
        gidx = lax.broadcasted_iota(jnp.int32, shape2, 0)
        keep_prev = jnp.logical_and(lane >= row, jnp.bitwise_and(gidx + phase, period - 1) != 0)
        s_p = jnp.where(keep_prev, _bdot_nt(qs, kp), NEG_INF)
        mx = jnp.max(jnp.maximum(s_c, s_p), axis=2, keepdims=True)
        p_p = jnp.exp(s_p - mx).astype(BF16)
    else:
        mx = jnp.max(s_c, axis=2, keepdims=True)
    p_c = jnp.exp(s_c - mx).astype(BF16)
    res = []
    for hh in range(2):
        rows = slice(hh * blk, (hh + 1) * blk)
        if kp is not None:
            res.append(_bdot(jnp.concatenate([p_c[:, rows], p_p[:, rows]], axis=2),
                             jnp.concatenate([voc[hh], vop[hh]], axis=1)))
        else:
            res.append(_bdot(p_c[:, rows], voc[hh]))
    hi = lax.broadcasted_iota(jnp.int32, q.shape, 2) >= ATT_HD
    return (jnp.where(hi, res[1], res[0]), jnp.where(hi, res[0], res[1]),
            jnp.where(hi, mx[:, blk:], mx[:, :blk]))


def _attn_kernel(*refs, dils, seq, gb):
    n_g = len(dils)
    qkv = refs[:3 * n_g]
    o_ref = refs[3 * n_g]
    num_ref, den_ref, max_ref = refs[3 * n_g + 1:3 * n_g + 4]
    blk = ATT_BLOCK
    span = gb * blk
    order = sorted(range(n_g), key=lambda g: -dils[g])
    assert dils[order[-1]] == 1
    plane = {g: k for k, g in enumerate(order[:-1])}

    def mix(dst, num, den, mx):
        nums = [num] + [num_ref[k, dst, :] for k in range(n_g - 1)]
        dens = [den] + [den_ref[k, dst, :] for k in range(n_g - 1)]
        ms = [mx] + [max_ref[k, dst, :] for k in range(n_g - 1)]
        top = functools.reduce(jnp.maximum, ms)
        ws = [jnp.exp(m - top) for m in ms]
        total = functools.reduce(jnp.add, [w * pltpu.roll(d, ATT_HD, axis=1) for w, d in zip(ws, dens)])
        inv = 1.0 / total
        acc = functools.reduce(jnp.add, [(w * inv) * n for w, n in zip(ws, nums)])
        o_ref[dst, :] = acc.astype(BF16)

    for g in order:
        dil = dils[g]
        q_ref, k_ref, v_ref = qkv[3 * g:3 * g + 3]
        nb = seq // dil // blk
        assert nb % gb == 0 or gb % nb == 0
        runs = max(1, gb // nb)
        per = gb // runs

        def body(c, carry, q_ref=q_ref, k_ref=k_ref, v_ref=v_ref, nb=nb, dil=dil, g=g, runs=runs, per=per):
            r0 = (c * gb) // nb
            n0 = (c * gb) % nb
            if runs == 1:
                cur = pl.ds(pl.multiple_of(n0 * blk, blk), span)
                load = lambda ref: ref[r0, cur, :].reshape(gb, blk, LANES)
            else:
                load = lambda ref: ref[pl.ds(r0, runs)].reshape(gb, blk, LANES)
            q, kc = load(q_ref), load(k_ref)
            voc = _value_operands(load(v_ref))
            if nb == 1:
                num, den, mx = _attn_blocks(q, kc, voc, None, None, 0, 1)
            else:
                if runs == 1:
                    first = pl.ds(pl.multiple_of(jnp.maximum(n0 - 1, 0) * blk, blk), blk)
                    kf = k_ref[r0, first, :][None]
                    vof = _value_operands(v_ref[r0, first, :][None])
                else:
                    kf, vof = kc[:1], [voc[0][:1], voc[1][:1]]
                kp = jnp.concatenate([kf, kc[:gb - 1]], axis=0)
                vop = [jnp.concatenate([vof[hh], voc[hh][:gb - 1]], axis=0) for hh in range(2)]
                num, den, mx = _attn_blocks(q, kc, voc, kp, vop, n0, nb)
            if g == order[-1]:
                for t in range(gb):
                    mix(pl.ds(pl.multiple_of((n0 + t) * blk, blk), blk), num[t], den[t], mx[t])
                return carry
            for t in range(runs):
                rows = slice(t * per, (t + 1) * per)
                dst = pl.ds((r0 + t) + dil * n0 * blk, per * blk, stride=dil)
                num_ref[plane[g], dst, :] = num[rows].reshape(per * blk, LANES)
                den_ref[plane[g], dst, :] = den[rows].reshape(per * blk, LANES)
                max_ref[plane[g], dst, :] = mx[rows].reshape(per * blk, LANES)
            return carry

        lax.fori_loop(0, seq // span, body, 0, unroll=True)


def _attention(qkvs, *, batch, seq, dils, width, gb=8):
    n_pairs = width // LANES
    args, specs = [], []
    for arr, dil in zip(qkvs, dils):
        length = seq // dil
        for part in range(3):
            args.append(arr)
            specs.append(pl.BlockSpec((dil, length, LANES),
                                      lambda b, hp, part=part: (0, b, part * n_pairs + hp)))
    n_g = len(dils)
    vmem = (2 * 3 * n_g * seq * LANES * 2 + 3 * n_g * seq * LANES * 4 + 2 * seq * LANES * 2
            + 32 * gb * ATT_BLOCK * LANES * 4 + (8 << 20))
    return pl.pallas_call(
        functools.partial(_attn_kernel, dils=tuple(dils), seq=seq, gb=gb),
        grid=(batch, n_pairs),
        in_specs=specs,
        out_specs=pl.BlockSpec((None, seq, LANES), lambda b, hp: (b, 0, hp)),
        out_shape=jax.ShapeDtypeStruct((batch, seq, width), BF16),
        scratch_shapes=[pltpu.VMEM((n_g - 1, seq, LANES), F32)] * 3,
        compiler_params=_params(("parallel", "parallel"), vmem),
        name="dilated_attention",
    )(*args)


def _log_sigmoid(z):
    return jnp.minimum(z, 0.0) - jnp.log1p(jnp.exp(-jnp.abs(z)))


def _mlstm_proj_kernel(x_ref, g_ref, wq_ref, wkt_ref, wv_ref, wo_ref, wgc_ref, wgr_ref, bc_ref, br_ref,
                       q_ref, kt_ref, v_ref, og_ref, gc_ref, gr_ref):
    xn = _rms(x_ref[...], g_ref[...]).astype(BF16)
    q_ref[...] = _dot(xn, wq_ref[...]).astype(BF16)
    kt_ref[...] = (_dot_nt(wkt_ref[...], xn) * (ML_DQK ** -0.5)).astype(BF16)
    v_ref[...] = _dot(xn, wv_ref[...]).astype(BF16)
    og_ref[...] = _dot(xn, wo_ref[...])
    zc = _dot(xn, wgc_ref[...]) + bc_ref[...]
    lane = lax.broadcasted_iota(jnp.int32, zc.shape, 1)
    gc_ref[...] = jnp.where(lane >= ML_HEADS, _log_sigmoid(zc), zc)
    zr = _dot_nt(wgr_ref[...], xn) + br_ref[...]
    rowi = lax.broadcasted_iota(jnp.int32, zr.shape, 0)
    gr_ref[...] = jnp.where(rowi >= ML_HEADS, _log_sigmoid(zr), zr)


def _mlstm_proj(x, g_pre, w_in, b_i, b_f, *, tm=1024):
    m, d = x.shape
    nh = ML_HEADS
    o1 = nh * ML_DQK
    o2 = 2 * o1
    o3 = o2 + nh * ML_DV
    o4 = o3 + nh * ML_DV
    wq = w_in[:, :o1].astype(BF16)
    wkt = w_in[:, o1:o2].T.astype(BF16)
    wv = w_in[:, o2:o3].astype(BF16)
    wo = w_in[:, o3:o4].astype(BF16)
    wg = w_in[:, o4:]
    wgc = jnp.pad(wg, ((0, 0), (0, LANES - 2 * nh))).astype(BF16)
    wgr = wg.T.astype(BF16)
    bias = jnp.concatenate([b_i, b_f]).astype(F32)
    bc = jnp.pad(bias, (0, LANES - 2 * nh))[None, :]
    br = bias[:, None]
    tm = min(tm, m)
    assert m % tm == 0 and 2 * nh == SUBLANES
    dv = nh * ML_DV
    vmem = (2 * tm * d * 4 + 2 * d * (2 * o1 + 2 * dv + LANES + SUBLANES) * 2
            + 2 * tm * (2 * o1 * 2 + dv * 2 + dv * 4 + LANES * 4 + SUBLANES * 4)
            + tm * (2 * o1 + 2 * dv) * 4 + tm * d * 6 + (4 << 20))
    full = lambda shape: pl.BlockSpec(shape, lambda i: (0, 0))
    return pl.pallas_call(
        _mlstm_proj_kernel,
        grid=(m // tm,),
        in_specs=[
            pl.BlockSpec((tm, d), lambda i: (i, 0)),
            full((1, d)), full((d, o1)), full((o1, d)), full((d, dv)), full((d, dv)),
            full((d, LANES)), full((2 * nh, d)), full((1, LANES)), full((2 * nh, 1)),
        ],
        out_specs=[
            pl.BlockSpec((tm, o1), lambda i: (i, 0)),
            pl.BlockSpec((o1, tm), lambda i: (0, i)),
            pl.BlockSpec((tm, dv), lambda i: (i, 0)),
            pl.BlockSpec((tm, dv), lambda i: (i, 0)),
            pl.BlockSpec((tm, LANES), lambda i: (i, 0)),
            pl.BlockSpec((2 * nh, tm), lambda i: (0, i)),
        ],
        out_shape=[
            jax.ShapeDtypeStruct((m, o1), BF16),
            jax.ShapeDtypeStruct((o1, m), BF16),
            jax.ShapeDtypeStruct((m, dv), BF16),
            jax.ShapeDtypeStruct((m, dv), F32),
            jax.ShapeDtypeStruct((m, LANES), F32),
            jax.ShapeDtypeStruct((2 * nh, m), F32),
        ],
        compiler_params=_params(("parallel",), vmem),
        name="mlstm_proj",
    )(x, g_pre, wq, wkt, wv, wo, wgc, wgr, bc, br)


def _mlstm_kernel(q_ref, kt_ref, v_ref, og_ref, gc_ref, gr_ref, o_ref, st_ref, *, seq):
    nh, lc, dqk, dv = ML_HEADS, ML_CHUNK, ML_DQK, ML_DV
    st_ref[...] = jnp.zeros(st_ref.shape, F32)
    row = lax.broadcasted_iota(jnp.int32, (lc, lc), 0)
    col = lax.broadcasted_iota(jnp.int32, (lc, lc), 1)
    causal = col <= row
    tril = causal.astype(F32)
    triu = (row <= col).astype(F32)
    lane0 = lax.broadcasted_iota(jnp.int32, (lc, LANES), 1) == 0

    def body(j, ms):
        pos = pl.ds(pl.multiple_of(j * lc, lc), lc)
        gr_c = gr_ref[:, pos]
        gc_c = gc_ref[pos, :]
        b_rows = _dot_f32(gr_c, triu)
        b_cols = _dot_f32(tril, gc_c)
        new_ms = []
        for h in range(nh):
            m_h = ms[h]
            g_row = gr_c[h:h + 1, :] - b_rows[nh + h:nh + h + 1, :]
            b_col = b_cols[:, nh + h:nh + h + 1]
            g_col = gc_c[:, h:h + 1] - b_col
            dmat = jnp.where(causal, g_row, NEG_INF)
            mu = jnp.maximum(m_h, jnp.max(dmat, axis=1, keepdims=True))
            w = jnp.exp(dmat - mu)
            wi = jnp.exp(m_h - mu)
            emt = jnp.exp(-(b_col + mu))
            qh = q_ref[pos, h * dqk:(h + 1) * dqk]
            kth = kt_ref[h * dqk:(h + 1) * dqk, pos]
            vh = v_ref[pos, h * dv:(h + 1) * dv]
            a = _dot(qh, kth) * w
            state = st_ref[h]
            qc = _dot(qh, state.astype(BF16))
            num = _dot(a.astype(BF16), vh) + wi * qc[:, :dv]
            den = jnp.sum(a, axis=1, keepdims=True) + wi * qc[:, dv:dv + 1]
            hval = num / jnp.maximum(jnp.abs(den), emt)
            gate = jax.nn.sigmoid(og_ref[pos, h * dv:(h + 1) * dv])
            o_ref[pos, h * dv:(h + 1) * dv] = (hval * gate).astype(BF16)
            mu_last = mu[lc - 1:lc, :]
            we = jnp.exp(g_col - mu_last)
            decay = jnp.exp(m_h - mu_last)
            vaug = jnp.concatenate([vh.astype(F32) * we, jnp.where(lane0, we, 0.0)], axis=1)
            st_ref[h] = decay * state + _dot(kth, vaug.astype(BF16))
            new_ms.append(b_col[lc - 1:lc, :] + mu_last)
        return tuple(new_ms)

    lax.fori_loop(0, seq // lc, body, tuple(jnp.zeros((1, 1), F32) for _ in range(nh)))


def _mlstm_scan(q, kt, v, og, gc, gr, *, batch, seq):
    nh, dqk, dv = ML_HEADS, ML_DQK, ML_DV
    assert seq % ML_CHUNK == 0
    vmem = (2 * seq * (2 * nh * dqk * 2 + nh * dv * 2 + nh * dv * 4 + LANES * 4 + SUBLANES * 4 + nh * dv * 2)
            + nh * dqk * (dv + LANES) * 4 + (8 << 20))
    return pl.pallas_call(
        functools.partial(_mlstm_kernel, seq=seq),
        grid=(batch,),
        in_specs=[
            pl.BlockSpec((seq, nh * dqk), lambda b: (b, 0)),
            pl.BlockSpec((nh * dqk, seq), lambda b: (0, b)),
            pl.BlockSpec((seq, nh * dv), lambda b: (b, 0)),
            pl.BlockSpec((seq, nh * dv), lambda b: (b, 0)),
            pl.BlockSpec((seq, LANES), lambda b: (b, 0)),
            pl.BlockSpec((2 * nh, seq), lambda b: (0, b)),
        ],
        out_specs=pl.BlockSpec((seq, nh * dv), lambda b: (b, 0)),
        out_shape=jax.ShapeDtypeStruct((batch * seq, nh * dv), BF16),
        scratch_shapes=[pltpu.VMEM((nh, dqk, dv + LANES), F32)],
        compiler_params=_params(("parallel",), vmem),
        name="mlstm_scan",
    )(q, kt, v, og, gc, gr)


def kernel(x, norm_g, ffn1_wg, ffn1_wu, ffn1_wd, ffn2_wg, ffn2_wu, ffn2_wd,
           conv_w_in, conv_k, conv_w_out, attn_w_in, attn_w_out,
           mlstm_w_in, mlstm_b_i, mlstm_b_f, mlstm_w_out):
    batch, seq, d = x.shape
    depth = norm_g.shape[0]
    h = x.reshape(batch * seq, d)
    dils = tuple(dil for _, dil in DILATION_PAIRS)
    width = ATT_HEADS * ATT_HD
    for i in range(depth):
        g = [norm_g[i, k][None, :] for k in range(norm_g.shape[1])]
        h = _ffn(h, g[0], g[1], ffn1_wg, ffn1_wu, ffn1_wd, layer=i)
        mixer, j = i % N_MIXERS, i // N_MIXERS
        mix = None
        if mixer == 0:
            h = _conv_mixer(h, g[2], g[3], conv_w_in[j].astype(BF16), conv_k[j],
                            conv_w_out[j].astype(BF16), seq=seq)
        elif mixer == 1:
            w_in = attn_w_in[j].astype(BF16)
            qkvs = [_qkv_proj(h, g[2], w_in, group=gi, width=width, batch=batch, seq=seq, dil=dil)
                    for gi, dil in enumerate(dils)]
            att = _attention(qkvs, batch=batch, seq=seq, dils=dils, width=width)
            mix = (att.reshape(batch * seq, width), attn_w_out, j, g[3])
        else:
            q, kt, v, og, gc, gr = _mlstm_proj(h, g[2], mlstm_w_in[j], mlstm_b_i[j], mlstm_b_f[j])
            hg = _mlstm_scan(q, kt, v, og, gc, gr, batch=batch, seq=seq)
            mix = (hg, mlstm_w_out, j, g[3])
        h = _ffn(h, g[4], g[5], ffn2_wg, ffn2_wu, ffn2_wd, layer=i, mix=mix)
    return h.reshape(batch, seq, d)
```

```python
import functools
import math

import jax
import jax.numpy as jnp
from jax import lax
from jax.experimental import pallas as pl
from jax.experimental.pallas import tpu as pltpu

LANES = 128
SUBLANES = 8
V7X_VMEM_BYTES = 64 * 1024 * 1024

EPS = 1e-6
CONV_W = 3
ATT_HEADS = 16
ATT_HD = 64
DILATION_PAIRS = ((128, 1), (512, 4), (2048, 16))
ATT_BLOCK = 128
ROPE_THETA = 500000.0
ROT_DIM = ATT_HD // 4
ML_HEADS = 4
ML_DQK = 128
ML_DV = 256
ML_CHUNK = 256
N_MIXERS = 3

BF16 = jnp.bfloat16
F32 = jnp.float32
NEG_INF = float("-inf")


def _params(semantics, vmem_bytes):
    return pltpu.CompilerParams(dimension_semantics=semantics,
                                vmem_limit_bytes=min(int(vmem_bytes), V7X_VMEM_BYTES - (4 << 20)))


def _rms(x32, g_row):
    ms = jnp.mean(x32 * x32, axis=-1, keepdims=True)
    return (x32 * lax.rsqrt(ms + EPS)) * g_row


def _dot(a, b):
    return jnp.dot(a, b, preferred_element_type=F32)


def _dot_nt(a, b):
    return lax.dot_general(a, b, (((1,), (1,)), ((), ())), preferred_element_type=F32)


def _dot_f32(a, b):
    return jnp.dot(a, b, preferred_element_type=F32, precision=lax.Precision.HIGHEST)


def _fill_bf16(jobs):
    chunks = []
    used = {}
    for src, dst, stage, sems in jobs:
        rc = stage.shape[1]
        for c in range(src.shape[0] // rc):
            slot = used.get(id(stage), 0) % 2
            used[id(stage)] = used.get(id(stage), 0) + 1
            rows = slice(c * rc, (c + 1) * rc)
            copy = pltpu.make_async_copy(src.at[rows, :], stage.at[slot], sems.at[slot])
            chunks.append((copy, dst, rows, stage, slot))
    chunks[0][0].start()
    for n, (copy, dst, rows, stage, slot) in enumerate(chunks):
        if n + 1 < len(chunks):
            chunks[n + 1][0].start()
        copy.wait()
        dst[rows, :] = stage[slot].astype(BF16)


def _ffn_kernel(*refs, cw, layer, mix_layer):
    if mix_layer is None:
        (x_ref, gpre_ref, gpost_ref, wg_hbm, wu_hbm, wd_hbm, o_ref,
         wg_ref, wu_ref, wd_ref, acc_ref, stage_g, stage_u, stage_d, sem_g, sem_u, sem_d) = refs
    else:
        (x_ref, gpre_ref, gpost_ref, wg_hbm, wu_hbm, wd_hbm, a_ref, wmix_hbm, gmix_ref, o_ref,
         wg_ref, wu_ref, wd_ref, acc_ref, stage_g, stage_u, stage_d, sem_g, sem_u, sem_d,
         wmix_ref, stage_mix, sem_mix) = refs
    ff = wg_ref.shape[1]
    n_chunks = ff // cw

    def copies(c):
        slot, cols = c % 2, pl.ds(pl.multiple_of(c * cw, cw), cw)
        return [pltpu.make_async_copy(wg_hbm.at[layer, :, cols], stage_g.at[slot], sem_g.at[slot]),
                pltpu.make_async_copy(wu_hbm.at[layer, :, cols], stage_u.at[slot], sem_u.at[slot]),
                pltpu.make_async_copy(wd_hbm.at[layer, cols, :], stage_d.at[slot], sem_d.at[slot])]

    def chunk(xn, cols):
        g = _dot(xn, wg_ref[:, cols])
        u = _dot(xn, wu_ref[:, cols])
        h = ((g * jax.nn.sigmoid(g)) * u).astype(BF16)
        return _dot(h, wd_ref[cols, :])

    def tile(first):
        if first:
            for copy in copies(0):
                copy.start()
            if mix_layer is not None:
                _fill_bf16([(wmix_hbm.at[mix_layer], wmix_ref, stage_mix, sem_mix)])
        if mix_layer is None:
            x_in = x_ref
        else:
            o_ref[...] = x_ref[...] + _rms(_dot(a_ref[...], wmix_ref[...]), gmix_ref[...])
            x_in = o_ref
        xn = _rms(x_in[...], gpre_ref[...]).astype(BF16)
        if first:
            acc_ref[...] = jnp.zeros(acc_ref.shape, F32)

            def body(c, carry):
                @pl.when(c + 1 < n_chunks)
                def _():
                    for copy in copies(c + 1):
                        copy.start()

                for copy in copies(c):
                    copy.wait()
                slot, cols = c % 2, pl.ds(pl.multiple_of(c * cw, cw), cw)
                wg_ref[:, cols] = stage_g[slot].astype(BF16)
                wu_ref[:, cols] = stage_u[slot].astype(BF16)
                wd_ref[cols, :] = stage_d[slot].astype(BF16)
                acc_ref[...] += chunk(xn, cols)
                return carry

            lax.fori_loop(0, n_chunks, body, 0)
        else:
            for c in range(n_chunks):
                part = chunk(xn, slice(c * cw, (c + 1) * cw))
                if c == 0:
                    acc_ref[...] = part
                else:
                    acc_ref[...] += part
        o_ref[...] = x_in[...] + _rms(acc_ref[...], 0.5 * gpost_ref[...])

    pl.when(pl.program_id(0) == 0)(functools.partial(tile, True))
    pl.when(pl.program_id(0) > 0)(functools.partial(tile, False))


def _resident(shape):
    return pl.BlockSpec(shape, lambda *_: (0,) * len(shape), pipeline_mode=pl.Buffered(1))


def _ffn(x, g_pre, g_post, wg, wu, wd, *, layer, mix=None, tm=1024, cw=256, stage_rows=128):
    m, d = x.shape
    ff = wg.shape[2]
    tm = min(tm, m)
    assert m % tm == 0 and ff % cw == 0
    vmem = (4 * tm * d * 4
            + 3 * d * ff * 2
            + 3 * 2 * d * cw * 4
            + tm * d * (2 + 4 + 4)
            + 2 * tm * cw * (4 + 4 + 2 + 4)
            + (4 << 20))
    hbm = pl.BlockSpec(memory_space=pl.ANY)
    args = [x, g_pre, g_post, wg, wu, wd]
    in_specs = [pl.BlockSpec((tm, d), lambda i: (i, 0)), _resident((1, d)), _resident((1, d)), hbm, hbm, hbm]
    scratch = [
        pltpu.VMEM((d, ff), BF16), pltpu.VMEM((d, ff), BF16), pltpu.VMEM((ff, d), BF16),
        pltpu.VMEM((tm, d), F32),
        pltpu.VMEM((2, d, cw), F32), pltpu.VMEM((2, d, cw), F32), pltpu.VMEM((2, cw, d), F32),
        pltpu.SemaphoreType.DMA((2,)), pltpu.SemaphoreType.DMA((2,)), pltpu.SemaphoreType.DMA((2,)),
    ]
    mix_layer = None
    if mix is not None:
        a, w_mix, mix_layer, g_mix = mix
        k = a.shape[1]
        assert k % stage_rows == 0
        args += [a, w_mix, g_mix]
        in_specs += [pl.BlockSpec((tm, k), lambda i: (i, 0)), hbm, _resident((1, d))]
        scratch += [pltpu.VMEM((k, d), BF16), pltpu.VMEM((2, stage_rows, d), F32), pltpu.SemaphoreType.DMA((2,))]
        vmem += 2 * tm * k * 2 + k * d * 2 + 2 * stage_rows * d * 4 + tm * d * 4
    return pl.pallas_call(
        functools.partial(_ffn_kernel, cw=cw, layer=layer, mix_layer=mix_layer),
        grid=(m // tm,),
        in_specs=in_specs,
        out_specs=pl.BlockSpec((tm, d), lambda i: (i, 0)),
        out_shape=jax.ShapeDtypeStruct((m, d), F32),
        scratch_shapes=scratch,
        compiler_params=_params(("arbitrary",), vmem),
        name="ffn" if mix is None else "outproj_ffn",
    )(*args)


def _conv_kernel(x_ref, gpre_ref, gpost_ref, win_ref, k_ref, wout_ref, o_ref, ext_ref, *, tiles_per_seq, nsub):
    i = pl.program_id(0)
    tm, d = x_ref.shape
    halo = SUBLANES
    sub = tm // nsub

    @pl.when(i % tiles_per_seq == 0)
    def _():
        ext_ref[0:halo, :] = jnp.zeros((halo, d), F32)

    kk = k_ref[...]
    for s in range(nsub):
        lo = s * sub
        xs = x_ref[lo:lo + sub, :]
        xn = _rms(xs, gpre_ref[...]).astype(BF16)
        p = _dot(xn, win_ref[...])
        cu = p[:, d:2 * d] * p[:, 2 * d:3 * d]
        ext_ref[halo + lo:halo + lo + sub, :] = cu
        conv = (kk[0:1, :] * cu
                + kk[1:2, :] * ext_ref[halo - 1 + lo:halo - 1 + lo + sub, :]
                + kk[2:3, :] * ext_ref[halo - 2 + lo:halo - 2 + lo + sub, :])
        y = _dot((p[:, 0:d] * conv).astype(BF16), wout_ref[...])
        o_ref[lo:lo + sub, :] = xs + _rms(y, gpost_ref[...])
    ext_ref[0:halo, :] = ext_ref[tm:tm + halo, :]


def _conv_mixer(x, g_pre, g_post, w_in, k, w_out, *, seq, tm=1024, nsub=2):
    m, d = x.shape
    tm = min(tm, seq)
    assert seq % tm == 0 and m % seq == 0 and tm % nsub == 0
    sub = tm // nsub
    vmem = (4 * tm * d * 4 + (3 * d * d + d * d) * 2 + (tm + SUBLANES) * d * 4
            + nsub * (sub * 3 * d * 4 + 6 * sub * d * 4) + (4 << 20))
    return pl.pallas_call(
        functools.partial(_conv_kernel, tiles_per_seq=seq // tm, nsub=nsub),
        grid=(m // tm,),
        in_specs=[
            pl.BlockSpec((tm, d), lambda i: (i, 0)),
            _resident((1, d)),
            _resident((1, d)),
            _resident((d, 3 * d)),
            _resident((CONV_W, d)),
            _resident((d, d)),
        ],
        out_specs=pl.BlockSpec((tm, d), lambda i: (i, 0)),
        out_shape=jax.ShapeDtypeStruct((m, d), F32),
        scratch_shapes=[pltpu.VMEM((tm + SUBLANES, d), F32)],
        compiler_params=_params(("arbitrary",), vmem),
        name="conv_mixer",
    )(x, g_pre, g_post, w_in, k, w_out)


def _rope_tables(seq, dil, rows):
    length = seq // dil
    u = jnp.arange(rows, dtype=jnp.int32) % length
    pos = (u[None, :] * dil + jnp.arange(dil, dtype=jnp.int32)[:, None]).astype(F32)
    inv = ROPE_THETA ** (-jnp.arange(0, ROT_DIM, 2, dtype=F32) / ROT_DIM)
    ang = pos[:, :, None] * inv[None, None, :]
    cos, sin = jnp.cos(ang), jnp.sin(ang)
    half = ROT_DIM // 2
    dd = jnp.arange(LANES) % ATT_HD
    fi = dd % half
    cos_l = jnp.take(cos, fi, axis=-1)
    sin_l = jnp.take(sin, fi, axis=-1)
    c_tab = jnp.where(dd < ROT_DIM, cos_l, 1.0)
    s_lo = jnp.where(dd < half, -sin_l, 0.0)
    s_hi = jnp.where((dd >= half) & (dd < ROT_DIM), sin_l, 0.0)
    return c_tab.astype(F32), s_lo.astype(F32), s_hi.astype(F32)


def _qkv_kernel(x_ref, g_ref, w_ref, c_ref, slo_ref, shi_ref, o_ref, slab_ref, xp_ref, *, width, dil):
    tm, d = x_ref.shape
    per = tm // dil
    xn = _rms(x_ref[...], g_ref[...])
    if dil == 1:
        xp = xn.astype(BF16)
    else:
        for cb in range(d // LANES):
            slab_ref[cb] = xn[:, cb * LANES:(cb + 1) * LANES]
        for r in range(dil):
            for cb in range(d // LANES):
                piece = slab_ref[cb, pl.ds(r, per, stride=dil), :]
                xp_ref[r * per:(r + 1) * per, cb * LANES:(cb + 1) * LANES] = piece.astype(BF16)
        xp = xp_ref[...]
    p = _dot(xp, w_ref[...])
    c_tab = c_ref[...].reshape(tm, LANES)
    s_lo = slo_ref[...].reshape(tm, LANES)
    s_hi = shi_ref[...].reshape(tm, LANES)
    half = ROT_DIM // 2
    scale = ATT_HD ** -0.5
    for part in range(2):
        for cb in range(width // LANES):
            lo = part * width + cb * LANES
            blk = p[:, lo:lo + LANES]
            rot = (blk * c_tab
                   + pltpu.roll(blk, LANES - half, axis=1) * s_lo
                   + pltpu.roll(blk, half, axis=1) * s_hi)
            if part == 0:
                rot = rot * scale
            o_ref[:, :, lo:lo + LANES] = rot.astype(BF16).reshape(dil, per, LANES)
    o_ref[:, :, 2 * width:3 * width] = p[:, 2 * width:3 * width].astype(BF16).reshape(dil, per, width)


def _qkv_proj(x, g_pre, w, *, group, width, batch, seq, dil, tm=1024):
    m, d = x.shape
    tm = min(tm, seq)
    per = tm // dil
    assert seq % tm == 0 and tm % dil == 0 and per % (2 * SUBLANES) == 0
    c_tab, s_lo, s_hi = _rope_tables(seq, dil, seq // dil)
    tiles_per_seq = seq // tm
    tab_spec = pl.BlockSpec((dil, per, LANES), lambda i: (0, i % tiles_per_seq, 0))
    vmem = (2 * tm * d * 4 + d * 3 * width * 2 + 2 * tm * 3 * width * 2 + 6 * tm * LANES * 4
            + tm * 3 * width * 4 + tm * d * (4 + 4 + 2) + 4 * tm * width * 4 + (4 << 20))
    return pl.pallas_call(
        functools.partial(_qkv_kernel, width=width, dil=dil),
        grid=(m // tm,),
        in_specs=[
            pl.BlockSpec((tm, d), lambda i: (i, 0)),
            _resident((1, d)),
            pl.BlockSpec((d, 3 * width), lambda i: (0, group), pipeline_mode=pl.Buffered(1)),
            tab_spec, tab_spec, tab_spec,
        ],
        out_specs=pl.BlockSpec((dil, per, 3 * width), lambda i: (0, i, 0)),
        out_shape=jax.ShapeDtypeStruct((dil, m // dil, 3 * width), BF16),
        scratch_shapes=[pltpu.VMEM((d // LANES, tm, LANES), F32), pltpu.VMEM((tm, d), BF16)],
        compiler_params=_params(("parallel",), vmem),
        name=f"qkv_proj_d{dil}",
    )(x, g_pre, w, c_tab, s_lo, s_hi)


def _bdot_nt(a, b):
    return lax.dot_general(a, b, (((2,), (2,)), ((0,), (0,))), preferred_element_type=F32)


def _bdot(a, b):
    return lax.dot_general(a, b, (((2,), (1,)), ((0,), (0,))), preferred_element_type=F32)


def _value_operands(v):
    hi = lax.broadcasted_iota(jnp.int32, v.shape, 2).astype(F32).astype(v.dtype) >= ATT_HD
    one = jnp.ones_like(v)
    return jnp.where(hi, one, v), jnp.where(hi, v, one)


def _attn_blocks(q, kc, voc, kp, vop, phase, period):
    g, blk, _ = q.shape
    shape2 = (g, 2 * blk, LANES)
    row2 = lax.broadcasted_iota(jnp.int32, shape2, 1)
    lane = lax.broadcasted_iota(jnp.int32, shape2, 2)
    row = jnp.bitwise_and(row2, blk - 1)
    q2 = jnp.concatenate([q, q], axis=1)
    qs = jnp.where((row2 >= blk) == (lane >= ATT_HD), q2, jnp.zeros_like(q2))
    s_c = jnp.where(lane <= row, _bdot_nt(qs, kc), NEG_INF)
    if kp is not None:
        gidx = lax.broadcasted_iota(jnp.int32, shape2, 0)
        keep_prev = jnp.logical_and(lane >= row, jnp.bitwise_and(gidx + phase, period - 1) != 0)
        s_p = jnp.where(keep_prev, _bdot_nt(qs, kp), NEG_INF)
        mx = jnp.max(jnp.maximum(s_c, s_p), axis=2, keepdims=True)
        p_p = jnp.exp(s_p - mx).astype(BF16)
    else:
        mx = jnp.max(s_c, axis=2, keepdims=True)
    p_c = jnp.exp(s_c - mx).astype(BF16)
    res = []
    for hh in range(2):
        rows = slice(hh * blk, (hh + 1) * blk)
        if kp is not None:
            res.append(_bdot(jnp.concatenate([p_c[:, rows], p_p[:, rows]], axis=2),
                             jnp.concatenate([voc[hh], vop[hh]], axis=1)))
        else:
            res.append(_bdot(p_c[:, rows], voc[hh]))
    hi = lax.broadcasted_iota(jnp.int32, q.shape, 2) >= ATT_HD
    return (jnp.where(hi, res[1], res[0]), jnp.where(hi, res[0], res[1]),
            jnp.where(hi, mx[:, blk:], mx[:, :blk]))


def _attn_kernel(*refs, dils, seq, gb, pairs):
    n_g = len(dils)
    qkv = refs[:3 * n_g]
    o_ref = refs[3 * n_g]
    num_ref, den_ref, max_ref = refs[3 * n_g + 1:3 * n_g + 4]
    blk = ATT_BLOCK
    span = gb * blk
    order = sorted(range(n_g), key=lambda g: -dils[g])
    assert dils[order[-1]] == 1

    for hp in range(pairs):
        lanes = slice(hp * LANES, (hp + 1) * LANES)
        base = hp * (n_g - 1)
        plane = {g: base + k for k, g in enumerate(order[:-1])}

        def mix(dst, num, den, mx, lanes=lanes, base=base):
            nums = [num] + [num_ref[base + k, dst, :] for k in range(n_g - 1)]
            dens = [den] + [den_ref[base + k, dst, :] for k in range(n_g - 1)]
            ms = [mx] + [max_ref[base + k, dst, :] for k in range(n_g - 1)]
            top = functools.reduce(jnp.maximum, ms)
            ws = [jnp.exp(m - top) for m in ms]
            total = functools.reduce(jnp.add, [w * pltpu.roll(d, ATT_HD, axis=1) for w, d in zip(ws, dens)])
            inv = 1.0 / total
            acc = functools.reduce(jnp.add, [(w * inv) * n for w, n in zip(ws, nums)])
            o_ref[dst, lanes] = acc.astype(BF16)

        for g in order:
            dil = dils[g]
            q_ref, k_ref, v_ref = qkv[3 * g:3 * g + 3]
            nb = seq // dil // blk
            assert nb % gb == 0 or gb % nb == 0
            runs = max(1, gb // nb)
            per = gb // runs

            def body(c, carry, q_ref=q_ref, k_ref=k_ref, v_ref=v_ref, nb=nb, dil=dil, g=g, runs=runs, per=per,
                     lanes=lanes, plane=plane, mix=mix):
                r0 = (c * gb) // nb
                n0 = (c * gb) % nb
                if runs == 1:
                    cur = pl.ds(pl.multiple_of(n0 * blk, blk), span)
                    load = lambda ref: ref[r0, cur, lanes].reshape(gb, blk, LANES)
                else:
                    load = lambda ref: ref[pl.ds(r0, runs), :, lanes].reshape(gb, blk, LANES)
                q, kc = load(q_ref), load(k_ref)
                voc = _value_operands(load(v_ref))
                if nb == 1:
                    num, den, mx = _attn_blocks(q, kc, voc, None, None, 0, 1)
                else:
                    if runs == 1:
                        first = pl.ds(pl.multiple_of(jnp.maximum(n0 - 1, 0) * blk, blk), blk)
                        kf = k_ref[r0, first, lanes][None]
                        vof = _value_operands(v_ref[r0, first, lanes][None])
                    else:
                        kf, vof = kc[:1], [voc[0][:1], voc[1][:1]]
                    kp = jnp.concatenate([kf, kc[:gb - 1]], axis=0)
                    vop = [jnp.concatenate([vof[hh], voc[hh][:gb - 1]], axis=0) for hh in range(2)]
                    num, den, mx = _attn_blocks(q, kc, voc, kp, vop, n0, nb)
                if g == order[-1]:
                    for t in range(gb):
                        mix(pl.ds(pl.multiple_of((n0 + t) * blk, blk), blk), num[t], den[t], mx[t])
                    return carry
                for t in range(runs):
                    rows = slice(t * per, (t + 1) * per)
                    dst = pl.ds((r0 + t) + dil * n0 * blk, per * blk, stride=dil)
                    num_ref[plane[g], dst, :] = num[rows].reshape(per * blk, LANES)
                    den_ref[plane[g], dst, :] = den[rows].reshape(per * blk, LANES)
                    max_ref[plane[g], dst, :] = mx[rows].reshape(per * blk, LANES)
                return carry

            lax.fori_loop(0, seq // span, body, 0, unroll=True)


def _attention(qkvs, *, batch, seq, dils, width, gb=8, pairs=2):
    lanes = pairs * LANES
    steps = width // lanes
    args, specs = [], []
    for arr, dil in zip(qkvs, dils):
        length = seq // dil
        for part in range(3):
            args.append(arr)
            specs.append(pl.BlockSpec((dil, length, lanes),
                                      lambda b, hp, part=part: (0, b, part * steps + hp)))
    n_g = len(dils)
    vmem = (2 * 3 * n_g * seq * lanes * 2 + 3 * pairs * (n_g - 1) * seq * LANES * 4 + 2 * seq * lanes * 2
            + 32 * gb * ATT_BLOCK * LANES * 4 + (8 << 20))
    return pl.pallas_call(
        functools.partial(_attn_kernel, dils=tuple(dils), seq=seq, gb=gb, pairs=pairs),
        grid=(batch, steps),
        in_specs=specs,
        out_specs=pl.BlockSpec((None, seq, lanes), lambda b, hp: (b, 0, hp)),
        out_shape=jax.ShapeDtypeStruct((batch, seq, width), BF16),
        scratch_shapes=[pltpu.VMEM((pairs * (n_g - 1), seq, LANES), F32)] * 3,
        compiler_params=_params(("parallel", "parallel"), vmem),
        name="dilated_attention",
    )(*args)


def _log_sigmoid(z):
    return jnp.minimum(z, 0.0) - jnp.log1p(jnp.exp(-jnp.abs(z)))


def _mlstm_proj_kernel(x_ref, g_ref, wq_ref, wkt_ref, wv_ref, wo_ref, wgc_ref, wgr_ref, bc_ref, br_ref,
                       q_ref, kt_ref, v_ref, og_ref, gc_ref, gr_ref):
    xn = _rms(x_ref[...], g_ref[...]).astype(BF16)
    q_ref[...] = _dot(xn, wq_ref[...]).astype(BF16)
    kt_ref[...] = (_dot_nt(wkt_ref[...], xn) * (ML_DQK ** -0.5)).astype(BF16)
    v_ref[...] = _dot(xn, wv_ref[...]).astype(BF16)
    og_ref[...] = _dot(xn, wo_ref[...])
    zc = _dot(xn, wgc_ref[...]) + bc_ref[...]
    lane = lax.broadcasted_iota(jnp.int32, zc.shape, 1)
    gc_ref[...] = jnp.where(lane >= ML_HEADS, _log_sigmoid(zc), zc)
    zr = _dot_nt(wgr_ref[...], xn) + br_ref[...]
    rowi = lax.broadcasted_iota(jnp.int32, zr.shape, 0)
    gr_ref[...] = jnp.where(rowi >= ML_HEADS, _log_sigmoid(zr), zr)


def _mlstm_proj(x, g_pre, w_in, b_i, b_f, *, tm=1024):
    m, d = x.shape
    nh = ML_HEADS
    o1 = nh * ML_DQK
    o2 = 2 * o1
    o3 = o2 + nh * ML_DV
    o4 = o3 + nh * ML_DV
    wq = w_in[:, :o1].astype(BF16)
    wkt = w_in[:, o1:o2].T.astype(BF16)
    wv = w_in[:, o2:o3].astype(BF16)
    wo = w_in[:, o3:o4].astype(BF16)
    wg = w_in[:, o4:]
    wgc = jnp.pad(wg, ((0, 0), (0, LANES - 2 * nh))).astype(BF16)
    wgr = wg.T.astype(BF16)
    bias = jnp.concatenate([b_i, b_f]).astype(F32)
    bc = jnp.pad(bias, (0, LANES - 2 * nh))[None, :]
    br = bias[:, None]
    tm = min(tm, m)
    assert m % tm == 0 and 2 * nh == SUBLANES
    dv = nh * ML_DV
    vmem = (2 * tm * d * 4 + 2 * d * (2 * o1 + 2 * dv + LANES + SUBLANES) * 2
            + 2 * tm * (2 * o1 * 2 + dv * 2 + dv * 4 + LANES * 4 + SUBLANES * 4)
            + tm * (2 * o1 + 2 * dv) * 4 + tm * d * 6 + (4 << 20))
    full = lambda shape: pl.BlockSpec(shape, lambda i: (0, 0))
    return pl.pallas_call(
        _mlstm_proj_kernel,
        grid=(m // tm,),
        in_specs=[
            pl.BlockSpec((tm, d), lambda i: (i, 0)),
            full((1, d)), full((d, o1)), full((o1, d)), full((d, dv)), full((d, dv)),
            full((d, LANES)), full((2 * nh, d)), full((1, LANES)), full((2 * nh, 1)),
        ],
        out_specs=[
            pl.BlockSpec((tm, o1), lambda i: (i, 0)),
            pl.BlockSpec((o1, tm), lambda i: (0, i)),
            pl.BlockSpec((tm, dv), lambda i: (i, 0)),
            pl.BlockSpec((tm, dv), lambda i: (i, 0)),
            pl.BlockSpec((tm, LANES), lambda i: (i, 0)),
            pl.BlockSpec((2 * nh, tm), lambda i: (0, i)),
        ],
        out_shape=[
            jax.ShapeDtypeStruct((m, o1), BF16),
            jax.ShapeDtypeStruct((o1, m), BF16),
            jax.ShapeDtypeStruct((m, dv), BF16),
            jax.ShapeDtypeStruct((m, dv), F32),
            jax.ShapeDtypeStruct((m, LANES), F32),
            jax.ShapeDtypeStruct((2 * nh, m), F32),
        ],
        compiler_params=_params(("parallel",), vmem),
        name="mlstm_proj",
    )(x, g_pre, wq, wkt, wv, wo, wgc, wgr, bc, br)


def _mlstm_kernel(q_ref, kt_ref, v_ref, og_ref, gc_ref, gr_ref, o_ref, st_ref, *, seq):
    nh, lc, dqk, dv = ML_HEADS, ML_CHUNK, ML_DQK, ML_DV
    st_ref[...] = jnp.zeros(st_ref.shape, F32)
    row = lax.broadcasted_iota(jnp.int32, (lc, lc), 0)
    col = lax.broadcasted_iota(jnp.int32, (lc, lc), 1)
    causal = col <= row
    tril = causal.astype(F32)
    triu = (row <= col).astype(F32)
    lane0 = lax.broadcasted_iota(jnp.int32, (lc, LANES), 1) == 0

    def body(j, ms):
        pos = pl.ds(pl.multiple_of(j * lc, lc), lc)
        gr_c = gr_ref[:, pos]
        gc_c = gc_ref[pos, :]
        b_rows = _dot_f32(gr_c, triu)
        b_cols = _dot_f32(tril, gc_c)
        new_ms = []
        for h in range(nh):
            m_h = ms[h]
            g_row = gr_c[h:h + 1, :] - b_rows[nh + h:nh + h + 1, :]
            b_col = b_cols[:, nh + h:nh + h + 1]
            g_col = gc_c[:, h:h + 1] - b_col
            dmat = jnp.where(causal, g_row, NEG_INF)
            mu = jnp.maximum(m_h, jnp.max(dmat, axis=1, keepdims=True))
            w = jnp.exp(dmat - mu)
            wi = jnp.exp(m_h - mu)
            emt = jnp.exp(-(b_col + mu))
            qh = q_ref[pos, h * dqk:(h + 1) * dqk]
            kth = kt_ref[h * dqk:(h + 1) * dqk, pos]
            vh = v_ref[pos, h * dv:(h + 1) * dv]
            a = _dot(qh, kth) * w
            state = st_ref[h]
            qc = _dot(qh, state.astype(BF16))
            num = _dot(a.astype(BF16), vh) + wi * qc[:, :dv]
            den = jnp.sum(a, axis=1, keepdims=True) + wi * qc[:, dv:dv + 1]
            hval = num / jnp.maximum(jnp.abs(den), emt)
            gate = jax.nn.sigmoid(og_ref[pos, h * dv:(h + 1) * dv])
            o_ref[pos, h * dv:(h + 1) * dv] = (hval * gate).astype(BF16)
            mu_last = mu[lc - 1:lc, :]
            we = jnp.exp(g_col - mu_last)
            decay = jnp.exp(m_h - mu_last)
            vaug = jnp.concatenate([vh.astype(F32) * we, jnp.where(lane0, we, 0.0)], axis=1)
            st_ref[h] = decay * state + _dot(kth, vaug.astype(BF16))
            new_ms.append(b_col[lc - 1:lc, :] + mu_last)
        return tuple(new_ms)

    lax.fori_loop(0, seq // lc, body, tuple(jnp.zeros((1, 1), F32) for _ in range(nh)))


def _mlstm_scan(q, kt, v, og, gc, gr, *, batch, seq):
    nh, dqk, dv = ML_HEADS, ML_DQK, ML_DV
    assert seq % ML_CHUNK == 0
    vmem = (2 * seq * (2 * nh * dqk * 2 + nh * dv * 2 + nh * dv * 4 + LANES * 4 + SUBLANES * 4 + nh * dv * 2)
            + nh * dqk * (dv + LANES) * 4 + (8 << 20))
    return pl.pallas_call(
        functools.partial(_mlstm_kernel, seq=seq),
        grid=(batch,),
        in_specs=[
            pl.BlockSpec((seq, nh * dqk), lambda b: (b, 0)),
            pl.BlockSpec((nh * dqk, seq), lambda b: (0, b)),
            pl.BlockSpec((seq, nh * dv), lambda b: (b, 0)),
            pl.BlockSpec((seq, nh * dv), lambda b: (b, 0)),
            pl.BlockSpec((seq, LANES), lambda b: (b, 0)),
            pl.BlockSpec((2 * nh, seq), lambda b: (0, b)),
        ],
        out_specs=pl.BlockSpec((seq, nh * dv), lambda b: (b, 0)),
        out_shape=jax.ShapeDtypeStruct((batch * seq, nh * dv), BF16),
        scratch_shapes=[pltpu.VMEM((nh, dqk, dv + LANES), F32)],
        compiler_params=_params(("parallel",), vmem),
        name="mlstm_scan",
    )(q, kt, v, og, gc, gr)


def kernel(x, norm_g, ffn1_wg, ffn1_wu, ffn1_wd, ffn2_wg, ffn2_wu, ffn2_wd,
           conv_w_in, conv_k, conv_w_out, attn_w_in, attn_w_out,
           mlstm_w_in, mlstm_b_i, mlstm_b_f, mlstm_w_out):
    batch, seq, d = x.shape
    depth = norm_g.shape[0]
    h = x.reshape(batch * seq, d)
    dils = tuple(dil for _, dil in DILATION_PAIRS)
    width = ATT_HEADS * ATT_HD
    for i in range(depth):
        g = [norm_g[i, k][None, :] for k in range(norm_g.shape[1])]
        h = _ffn(h, g[0], g[1], ffn1_wg, ffn1_wu, ffn1_wd, layer=i)
        mixer, j = i % N_MIXERS, i // N_MIXERS
        mix = None
        if mixer == 0:
            h = _conv_mixer(h, g[2], g[3], conv_w_in[j].astype(BF16), conv_k[j],
                            conv_w_out[j].astype(BF16), seq=seq)
        elif mixer == 1:
            w_in = attn_w_in[j].astype(BF16)
            qkvs = [_qkv_proj(h, g[2], w_in, group=gi, width=width, batch=batch, seq=seq, dil=dil)
                    for gi, dil in enumerate(dils)]
            att = _attention(qkvs, batch=batch, seq=seq, dils=dils, width=width)
            mix = (att.reshape(batch * seq, width), attn_w_out, j, g[3])
        else:
            q, kt, v, og, gc, gr = _mlstm_proj(h, g[2], mlstm_w_in[j], mlstm_b_i[j], mlstm_b_f[j])
            hg = _mlstm_scan(q, kt, v, og, gc, gr, batch=batch, seq=seq)
            mix = (hg, mlstm_w_out, j, g[3])
        h = _ffn(h, g[4], g[5], ffn2_wg, ffn2_wu, ffn2_wd, layer=i, mix=mix)
    return h.reshape(batch, seq, d)
```

```python
import functools
import math

import jax
import jax.numpy as jnp
from jax import lax
from jax.experimental import pallas as pl
from jax.experimental.pallas import tpu as pltpu

LANES = 128
SUBLANES = 8
V7X_VMEM_BYTES = 64 * 1024 * 1024

EPS = 1e-6
CONV_W = 3
ATT_HEADS = 16
ATT_HD = 64
DILATION_PAIRS = ((128, 1), (512, 4), (2048, 16))
ATT_BLOCK = 128
ROPE_THETA = 500000.0
ROT_DIM = ATT_HD // 4
ML_HEADS = 4
ML_DQK = 128
ML_DV = 256
ML_CHUNK = 256
N_MIXERS = 3

BF16 = jnp.bfloat16
F32 = jnp.float32
NEG_INF = float("-inf")


def _params(semantics, vmem_bytes):
    return pltpu.CompilerParams(dimension_semantics=semantics,
                                vmem_limit_bytes=min(int(vmem_bytes), V7X_VMEM_BYTES - (4 << 20)))


def _rms(x32, g_row):
    ms = jnp.mean(x32 * x32, axis=-1, keepdims=True)
    return (x32 * lax.rsqrt(ms + EPS)) * g_row


def _dot(a, b):
    return jnp.dot(a, b, preferred_element_type=F32)


def _dot_nt(a, b):
    return lax.dot_general(a, b, (((1,), (1,)), ((), ())), preferred_element_type=F32)


def _dot_f32(a, b):
    return jnp.dot(a, b, preferred_element_type=F32, precision=lax.Precision.HIGHEST)


def _fill_bf16(jobs):
    chunks = []
    used = {}
    for src, dst, stage, sems in jobs:
        rc = stage.shape[1]
        for c in range(src.shape[0] // rc):
            slot = used.get(id(stage), 0) % 2
            used[id(stage)] = used.get(id(stage), 0) + 1
            rows = slice(c * rc, (c + 1) * rc)
            copy = pltpu.make_async_copy(src.at[rows, :], stage.at[slot], sems.at[slot])
            chunks.append((copy, dst, rows, stage, slot))
    chunks[0][0].start()
    for n, (copy, dst, rows, stage, slot) in enumerate(chunks):
        if n + 1 < len(chunks):
            chunks[n + 1][0].start()
        copy.wait()
        dst[rows, :] = stage[slot].astype(BF16)


def _ffn_kernel(*refs, cw, layer, mix_layer):
    if mix_layer is None:
        (x_ref, gpre_ref, gpost_ref, wg_hbm, wu_hbm, wd_hbm, o_ref,
         wg_ref, wu_ref, wd_ref, acc_ref, stage_g, stage_u, stage_d, sem_g, sem_u, sem_d) = refs
    else:
        (x_ref, gpre_ref, gpost_ref, wg_hbm, wu_hbm, wd_hbm, a_ref, wmix_hbm, gmix_ref, o_ref,
         wg_ref, wu_ref, wd_ref, acc_ref, stage_g, stage_u, stage_d, sem_g, sem_u, sem_d,
         wmix_ref, stage_mix, sem_mix) = refs
    ff = wg_ref.shape[1]
    n_chunks = ff // cw

    def copies(c):
        slot, cols = c % 2, pl.ds(pl.multiple_of(c * cw, cw), cw)
        return [pltpu.make_async_copy(wg_hbm.at[layer, :, cols], stage_g.at[slot], sem_g.at[slot]),
                pltpu.make_async_copy(wu_hbm.at[layer, :, cols], stage_u.at[slot], sem_u.at[slot]),
                pltpu.make_async_copy(wd_hbm.at[layer, cols, :], stage_d.at[slot], sem_d.at[slot])]

    def chunk(xn, cols):
        g = _dot(xn, wg_ref[:, cols])
        u = _dot(xn, wu_ref[:, cols])
        h = ((g * jax.nn.sigmoid(g)) * u).astype(BF16)
        return _dot(h, wd_ref[cols, :])

    def tile(first):
        if first:
            for copy in copies(0):
                copy.start()
            if mix_layer is not None:
                _fill_bf16([(wmix_hbm.at[mix_layer], wmix_ref, stage_mix, sem_mix)])
        if mix_layer is None:
            x_in = x_ref
        else:
            o_ref[...] = x_ref[...] + _rms(_dot(a_ref[...], wmix_ref[...]), gmix_ref[...])
            x_in = o_ref
        xn = _rms(x_in[...], gpre_ref[...]).astype(BF16)
        if first:
            acc_ref[...] = jnp.zeros(acc_ref.shape, F32)

            def body(c, carry):
                @pl.when(c + 1 < n_chunks)
                def _():
                    for copy in copies(c + 1):
                        copy.start()

                for copy in copies(c):
                    copy.wait()
                slot, cols = c % 2, pl.ds(pl.multiple_of(c * cw, cw), cw)
                wg_ref[:, cols] = stage_g[slot].astype(BF16)
                wu_ref[:, cols] = stage_u[slot].astype(BF16)
                wd_ref[cols, :] = stage_d[slot].astype(BF16)
                acc_ref[...] += chunk(xn, cols)
                return carry

            lax.fori_loop(0, n_chunks, body, 0)
        else:
            for c in range(n_chunks):
                part = chunk(xn, slice(c * cw, (c + 1) * cw))
                if c == 0:
                    acc_ref[...] = part
                else:
                    acc_ref[...] += part
        o_ref[...] = x_in[...] + _rms(acc_ref[...], 0.5 * gpost_ref[...])

    pl.when(pl.program_id(0) == 0)(functools.partial(tile, True))
    pl.when(pl.program_id(0) > 0)(functools.partial(tile, False))


def _resident(shape):
    return pl.BlockSpec(shape, lambda *_: (0,) * len(shape), pipeline_mode=pl.Buffered(1))


def _ffn(x, g_pre, g_post, wg, wu, wd, *, layer, mix=None, tm=1024, cw=256, stage_rows=128):
    m, d = x.shape
    ff = wg.shape[2]
    tm = min(tm, m)
    assert m % tm == 0 and ff % cw == 0
    vmem = (4 * tm * d * 4
            + 3 * d * ff * 2
            + 3 * 2 * d * cw * 4
            + tm * d * (2 + 4 + 4)
            + 2 * tm * cw * (4 + 4 + 2 + 4)
            + (4 << 20))
    hbm = pl.BlockSpec(memory_space=pl.ANY)
    args = [x, g_pre, g_post, wg, wu, wd]
    in_specs = [pl.BlockSpec((tm, d), lambda i: (i, 0)), _resident((1, d)), _resident((1, d)), hbm, hbm, hbm]
    scratch = [
        pltpu.VMEM((d, ff), BF16), pltpu.VMEM((d, ff), BF16), pltpu.VMEM((ff, d), BF16),
        pltpu.VMEM((tm, d), F32),
        pltpu.VMEM((2, d, cw), F32), pltpu.VMEM((2, d, cw), F32), pltpu.VMEM((2, cw, d), F32),
        pltpu.SemaphoreType.DMA((2,)), pltpu.SemaphoreType.DMA((2,)), pltpu.SemaphoreType.DMA((2,)),
    ]
    mix_layer = None
    if mix is not None:
        a, w_mix, mix_layer, g_mix = mix
        k = a.shape[1]
        assert k % stage_rows == 0
        args += [a, w_mix, g_mix]
        in_specs += [pl.BlockSpec((tm, k), lambda i: (i, 0)), hbm, _resident((1, d))]
        scratch += [pltpu.VMEM((k, d), BF16), pltpu.VMEM((2, stage_rows, d), F32), pltpu.SemaphoreType.DMA((2,))]
        vmem += 2 * tm * k * 2 + k * d * 2 + 2 * stage_rows * d * 4 + tm * d * 4
    return pl.pallas_call(
        functools.partial(_ffn_kernel, cw=cw, layer=layer, mix_layer=mix_layer),
        grid=(m // tm,),
        in_specs=in_specs,
        out_specs=pl.BlockSpec((tm, d), lambda i: (i, 0)),
        out_shape=jax.ShapeDtypeStruct((m, d), F32),
        scratch_shapes=scratch,
        compiler_params=_params(("arbitrary",), vmem),
        name="ffn" if mix is None else "outproj_ffn",
    )(*args)


def _conv_kernel(x_ref, gpre_ref, gpost_ref, win_ref, k_ref, wout_ref, o_ref, ext_ref, *, tiles_per_seq, nsub):
    i = pl.program_id(0)
    tm, d = x_ref.shape
    halo = SUBLANES
    sub = tm // nsub

    @pl.when(i % tiles_per_seq == 0)
    def _():
        ext_ref[0:halo, :] = jnp.zeros((halo, d), F32)

    kk = k_ref[...]
    for s in range(nsub):
        lo = s * sub
        xs = x_ref[lo:lo + sub, :]
        xn = _rms(xs, gpre_ref[...]).astype(BF16)
        p = _dot(xn, win_ref[...])
        cu = p[:, d:2 * d] * p[:, 2 * d:3 * d]
        ext_ref[halo + lo:halo + lo + sub, :] = cu
        conv = (kk[0:1, :] * cu
                + kk[1:2, :] * ext_ref[halo - 1 + lo:halo - 1 + lo + sub, :]
                + kk[2:3, :] * ext_ref[halo - 2 + lo:halo - 2 + lo + sub, :])
        y = _dot((p[:, 0:d] * conv).astype(BF16), wout_ref[...])
        o_ref[lo:lo + sub, :] = xs + _rms(y, gpost_ref[...])
    ext_ref[0:halo, :] = ext_ref[tm:tm + halo, :]


def _conv_mixer(x, g_pre, g_post, w_in, k, w_out, *, seq, tm=1024, nsub=2):
    m, d = x.shape
    tm = min(tm, seq)
    assert seq % tm == 0 and m % seq == 0 and tm % nsub == 0
    sub = tm // nsub
    vmem = (4 * tm * d * 4 + (3 * d * d + d * d) * 2 + (tm + SUBLANES) * d * 4
            + nsub * (sub * 3 * d * 4 + 6 * sub * d * 4) + (4 << 20))
    return pl.pallas_call(
        functools.partial(_conv_kernel, tiles_per_seq=seq // tm, nsub=nsub),
        grid=(m // tm,),
        in_specs=[
            pl.BlockSpec((tm, d), lambda i: (i, 0)),
            _resident((1, d)),
            _resident((1, d)),
            _resident((d, 3 * d)),
            _resident((CONV_W, d)),
            _resident((d, d)),
        ],
        out_specs=pl.BlockSpec((tm, d), lambda i: (i, 0)),
        out_shape=jax.ShapeDtypeStruct((m, d), F32),
        scratch_shapes=[pltpu.VMEM((tm + SUBLANES, d), F32)],
        compiler_params=_params(("arbitrary",), vmem),
        name="conv_mixer",
    )(x, g_pre, g_post, w_in, k, w_out)


def _rope_tables(seq, dil, rows):
    length = seq // dil
    u = jnp.arange(rows, dtype=jnp.int32) % length
    pos = (u[None, :] * dil + jnp.arange(dil, dtype=jnp.int32)[:, None]).astype(F32)
    inv = ROPE_THETA ** (-jnp.arange(0, ROT_DIM, 2, dtype=F32) / ROT_DIM)
    ang = pos[:, :, None] * inv[None, None, :]
    cos, sin = jnp.cos(ang), jnp.sin(ang)
    half = ROT_DIM // 2
    dd = jnp.arange(LANES) % ATT_HD
    fi = dd % half
    cos_l = jnp.take(cos, fi, axis=-1)
    sin_l = jnp.take(sin, fi, axis=-1)
    c_tab = jnp.where(dd < ROT_DIM, cos_l, 1.0)
    s_lo = jnp.where(dd < half, -sin_l, 0.0)
    s_hi = jnp.where((dd >= half) & (dd < ROT_DIM), sin_l, 0.0)
    return c_tab.astype(F32), s_lo.astype(F32), s_hi.astype(F32)


def _qkv_kernel(x_ref, g_ref, w_ref, c_ref, slo_ref, shi_ref, o_ref, slab_ref, xp_ref, *, width, dil):
    tm, d = x_ref.shape
    per = tm // dil
    xn = _rms(x_ref[...], g_ref[...])
    if dil == 1:
        xp = xn.astype(BF16)
    else:
        for cb in range(d // LANES):
            slab_ref[cb] = xn[:, cb * LANES:(cb + 1) * LANES]
        for r in range(dil):
            for cb in range(d // LANES):
                piece = slab_ref[cb, pl.ds(r, per, stride=dil), :]
                xp_ref[r * per:(r + 1) * per, cb * LANES:(cb + 1) * LANES] = piece.astype(BF16)
        xp = xp_ref[...]
    p = _dot(xp, w_ref[...])
    c_tab = c_ref[...].reshape(tm, LANES)
    s_lo = slo_ref[...].reshape(tm, LANES)
    s_hi = shi_ref[...].reshape(tm, LANES)
    half = ROT_DIM // 2
    scale = ATT_HD ** -0.5 * math.log2(math.e)
    for part in range(2):
        for cb in range(width // LANES):
            lo = part * width + cb * LANES
            blk = p[:, lo:lo + LANES]
            rot = (blk * c_tab
                   + pltpu.roll(blk, LANES - half, axis=1) * s_lo
                   + pltpu.roll(blk, half, axis=1) * s_hi)
            if part == 0:
                rot = rot * scale
            o_ref[:, :, lo:lo + LANES] = rot.astype(BF16).reshape(dil, per, LANES)
    o_ref[:, :, 2 * width:3 * width] = p[:, 2 * width:3 * width].astype(BF16).reshape(dil, per, width)


def _qkv_proj(x, g_pre, w, *, group, width, batch, seq, dil, tm=1024):
    m, d = x.shape
    tm = min(tm, seq)
    per = tm // dil
    assert seq % tm == 0 and tm % dil == 0 and per % (2 * SUBLANES) == 0
    c_tab, s_lo, s_hi = _rope_tables(seq, dil, seq // dil)
    tiles_per_seq = seq // tm
    tab_spec = pl.BlockSpec((dil, per, LANES), lambda i: (0, i % tiles_per_seq, 0))
    vmem = (2 * tm * d * 4 + d * 3 * width * 2 + 2 * tm * 3 * width * 2 + 6 * tm * LANES * 4
            + tm * 3 * width * 4 + tm * d * (4 + 4 + 2) + 4 * tm * width * 4 + (4 << 20))
    return pl.pallas_call(
        functools.partial(_qkv_kernel, width=width, dil=dil),
        grid=(m // tm,),
        in_specs=[
            pl.BlockSpec((tm, d), lambda i: (i, 0)),
            _resident((1, d)),
            pl.BlockSpec((d, 3 * width), lambda i: (0, group), pipeline_mode=pl.Buffered(1)),
            tab_spec, tab_spec, tab_spec,
        ],
        out_specs=pl.BlockSpec((dil, per, 3 * width), lambda i: (0, i, 0)),
        out_shape=jax.ShapeDtypeStruct((dil, m // dil, 3 * width), BF16),
        scratch_shapes=[pltpu.VMEM((d // LANES, tm, LANES), F32), pltpu.VMEM((tm, d), BF16)],
        compiler_params=_params(("parallel",), vmem),
        name=f"qkv_proj_d{dil}",
    )(x, g_pre, w, c_tab, s_lo, s_hi)


def _bdot_nt(a, b):
    return lax.dot_general(a, b, (((2,), (2,)), ((0,), (0,))), preferred_element_type=F32)


def _bdot(a, b):
    return lax.dot_general(a, b, (((2,), (1,)), ((0,), (0,))), preferred_element_type=F32)


def _value_operand(v):
    hi = lax.broadcasted_iota(jnp.int32, v.shape, 2).astype(F32).astype(v.dtype) >= ATT_HD
    zero, one = jnp.zeros_like(v), jnp.ones_like(v)
    top = jnp.concatenate([jnp.where(hi, zero, v), jnp.where(hi, zero, one)], axis=2)
    bot = jnp.concatenate([jnp.where(hi, v, zero), jnp.where(hi, one, zero)], axis=2)
    return jnp.concatenate([top, bot], axis=1)


def _attn_blocks(q, kc, voc, kp, vop, phase, period):
    g, blk, _ = q.shape
    shape2 = (g, 2 * blk, LANES)
    row2 = lax.broadcasted_iota(jnp.int32, shape2, 1)
    lane = lax.broadcasted_iota(jnp.int32, shape2, 2)
    row = jnp.bitwise_and(row2, blk - 1)
    q2 = jnp.concatenate([q, q], axis=1)
    qs = jnp.where((row2 >= blk) == (lane >= ATT_HD), q2, jnp.zeros_like(q2))
    s_c = jnp.where(lane <= row, _bdot_nt(qs, kc), NEG_INF)
    if kp is not None:
        gidx = lax.broadcasted_iota(jnp.int32, shape2, 0)
        keep_prev = jnp.logical_and(lane >= row, jnp.bitwise_and(gidx + phase, period - 1) != 0)
        s_p = jnp.where(keep_prev, _bdot_nt(qs, kp), NEG_INF)
        mx = jnp.max(jnp.maximum(s_c, s_p), axis=2, keepdims=True)
        p_p = jnp.exp2(s_p - mx).astype(BF16)
    else:
        mx = jnp.max(s_c, axis=2, keepdims=True)
    p_c = jnp.exp2(s_c - mx).astype(BF16)
    if kp is not None:
        nd = _bdot(jnp.concatenate([p_c[:, :blk], p_c[:, blk:], p_p[:, :blk], p_p[:, blk:]], axis=2),
                   jnp.concatenate([voc, vop], axis=1))
    else:
        nd = _bdot(jnp.concatenate([p_c[:, :blk], p_c[:, blk:]], axis=2), voc)
    hi = lax.broadcasted_iota(jnp.int32, q.shape, 2) >= ATT_HD
    return nd[:, :, :LANES], nd[:, :, LANES:], jnp.where(hi, mx[:, blk:], mx[:, :blk])


def _attn_kernel(*refs, dils, seq, gb, pairs):
    n_g = len(dils)
    qkv = refs[:3 * n_g]
    o_ref = refs[3 * n_g]
    num_ref, den_ref, max_ref = refs[3 * n_g + 1:3 * n_g + 4]
    blk = ATT_BLOCK
    span = gb * blk
    order = sorted(range(n_g), key=lambda g: -dils[g])
    assert dils[order[-1]] == 1

    for hp in range(pairs):
        lanes = slice(hp * LANES, (hp + 1) * LANES)
        base = hp * (n_g - 1)
        plane = {g: base + k for k, g in enumerate(order[:-1])}

        def mix(dst, num, den, mx, lanes=lanes, base=base):
            nums = [num] + [num_ref[base + k, dst, :] for k in range(n_g - 1)]
            dens = [den] + [den_ref[base + k, dst, :] for k in range(n_g - 1)]
            ms = [mx] + [max_ref[base + k, dst, :] for k in range(n_g - 1)]
            top = functools.reduce(jnp.maximum, ms)
            ws = [jnp.exp2(m - top) for m in ms]
            total = functools.reduce(jnp.add, [w * d for w, d in zip(ws, dens)])
            inv = 1.0 / total
            acc = functools.reduce(jnp.add, [(w * inv) * n for w, n in zip(ws, nums)])
            o_ref[dst, lanes] = acc.astype(BF16)

        for g in order:
            dil = dils[g]
            q_ref, k_ref, v_ref = qkv[3 * g:3 * g + 3]
            nb = seq // dil // blk
            assert nb % gb == 0 or gb % nb == 0
            runs = max(1, gb // nb)
            per = gb // runs

            def body(c, carry, q_ref=q_ref, k_ref=k_ref, v_ref=v_ref, nb=nb, dil=dil, g=g, runs=runs, per=per,
                     lanes=lanes, plane=plane, mix=mix):
                r0 = (c * gb) // nb
                n0 = (c * gb) % nb
                if runs == 1:
                    cur = pl.ds(pl.multiple_of(n0 * blk, blk), span)
                    load = lambda ref: ref[r0, cur, lanes].reshape(gb, blk, LANES)
                else:
                    load = lambda ref: ref[pl.ds(r0, runs), :, lanes].reshape(gb, blk, LANES)
                q, kc = load(q_ref), load(k_ref)
                voc = _value_operand(load(v_ref))
                if nb == 1:
                    num, den, mx = _attn_blocks(q, kc, voc, None, None, 0, 1)
                else:
                    if runs == 1:
                        first = pl.ds(pl.multiple_of(jnp.maximum(n0 - 1, 0) * blk, blk), blk)
                        kf = k_ref[r0, first, lanes][None]
                        vof = _value_operand(v_ref[r0, first, lanes][None])
                    else:
                        kf, vof = kc[:1], voc[:1]
                    kp = jnp.concatenate([kf, kc[:gb - 1]], axis=0)
                    vop = jnp.concatenate([vof, voc[:gb - 1]], axis=0)
                    num, den, mx = _attn_blocks(q, kc, voc, kp, vop, n0, nb)
                if g == order[-1]:
                    for t in range(gb):
                        mix(pl.ds(pl.multiple_of((n0 + t) * blk, blk), blk), num[t], den[t], mx[t])
                    return carry
                for t in range(runs):
                    rows = slice(t * per, (t + 1) * per)
                    dst = pl.ds((r0 + t) + dil * n0 * blk, per * blk, stride=dil)
                    num_ref[plane[g], dst, :] = num[rows].reshape(per * blk, LANES)
                    den_ref[plane[g], dst, :] = den[rows].reshape(per * blk, LANES)
                    max_ref[plane[g], dst, :] = mx[rows].reshape(per * blk, LANES)
                return carry

            lax.fori_loop(0, seq // span, body, 0, unroll=True)


def _attention(qkvs, *, batch, seq, dils, width, gb=8, pairs=2):
    lanes = pairs * LANES
    steps = width // lanes
    args, specs = [], []
    for arr, dil in zip(qkvs, dils):
        length = seq // dil
        for part in range(3):
            args.append(arr)
            specs.append(pl.BlockSpec((dil, length, lanes),
                                      lambda b, hp, part=part: (0, b, part * steps + hp)))
    n_g = len(dils)
    vmem = (2 * 3 * n_g * seq * lanes * 2 + 3 * pairs * (n_g - 1) * seq * LANES * 4 + 2 * seq * lanes * 2
            + 32 * gb * ATT_BLOCK * LANES * 4 + (8 << 20))
    return pl.pallas_call(
        functools.partial(_attn_kernel, dils=tuple(dils), seq=seq, gb=gb, pairs=pairs),
        grid=(batch, steps),
        in_specs=specs,
        out_specs=pl.BlockSpec((None, seq, lanes), lambda b, hp: (b, 0, hp)),
        out_shape=jax.ShapeDtypeStruct((batch, seq, width), BF16),
        scratch_shapes=[pltpu.VMEM((pairs * (n_g - 1), seq, LANES), F32)] * 3,
        compiler_params=_params(("parallel", "parallel"), vmem),
        name="dilated_attention",
    )(*args)


def _log_sigmoid(z):
    return jnp.minimum(z, 0.0) - jnp.log1p(jnp.exp(-jnp.abs(z)))


def _mlstm_proj_kernel(x_ref, g_ref, wq_ref, wkt_ref, wv_ref, wo_ref, wgc_ref, wgr_ref, bc_ref, br_ref,
                       q_ref, kt_ref, v_ref, og_ref, gc_ref, gr_ref):
    xn = _rms(x_ref[...], g_ref[...]).astype(BF16)
    q_ref[...] = _dot(xn, wq_ref[...]).astype(BF16)
    kt_ref[...] = (_dot_nt(wkt_ref[...], xn) * (ML_DQK ** -0.5)).astype(BF16)
    v_ref[...] = _dot(xn, wv_ref[...]).astype(BF16)
    og_ref[...] = _dot(xn, wo_ref[...])
    zc = _dot(xn, wgc_ref[...]) + bc_ref[...]
    lane = lax.broadcasted_iota(jnp.int32, zc.shape, 1)
    gc_ref[...] = jnp.where(lane >= ML_HEADS, _log_sigmoid(zc), zc)
    zr = _dot_nt(wgr_ref[...], xn) + br_ref[...]
    rowi = lax.broadcasted_iota(jnp.int32, zr.shape, 0)
    gr_ref[...] = jnp.where(rowi >= ML_HEADS, _log_sigmoid(zr), zr)


def _mlstm_proj(x, g_pre, w_in, b_i, b_f, *, tm=1024):
    m, d = x.shape
    nh = ML_HEADS
    o1 = nh * ML_DQK
    o2 = 2 * o1
    o3 = o2 + nh * ML_DV
    o4 = o3 + nh * ML_DV
    wq = w_in[:, :o1].astype(BF16)
    wkt = w_in[:, o1:o2].T.astype(BF16)
    wv = w_in[:, o2:o3].astype(BF16)
    wo = w_in[:, o3:o4].astype(BF16)
    wg = w_in[:, o4:]
    wgc = jnp.pad(wg, ((0, 0), (0, LANES - 2 * nh))).astype(BF16)
    wgr = wg.T.astype(BF16)
    bias = jnp.concatenate([b_i, b_f]).astype(F32)
    bc = jnp.pad(bias, (0, LANES - 2 * nh))[None, :]
    br = bias[:, None]
    tm = min(tm, m)
    assert m % tm == 0 and 2 * nh == SUBLANES
    dv = nh * ML_DV
    vmem = (2 * tm * d * 4 + 2 * d * (2 * o1 + 2 * dv + LANES + SUBLANES) * 2
            + 2 * tm * (2 * o1 * 2 + dv * 2 + dv * 4 + LANES * 4 + SUBLANES * 4)
            + tm * (2 * o1 + 2 * dv) * 4 + tm * d * 6 + (4 << 20))
    full = lambda shape: pl.BlockSpec(shape, lambda i: (0, 0))
    return pl.pallas_call(
        _mlstm_proj_kernel,
        grid=(m // tm,),
        in_specs=[
            pl.BlockSpec((tm, d), lambda i: (i, 0)),
            full((1, d)), full((d, o1)), full((o1, d)), full((d, dv)), full((d, dv)),
            full((d, LANES)), full((2 * nh, d)), full((1, LANES)), full((2 * nh, 1)),
        ],
        out_specs=[
            pl.BlockSpec((tm, o1), lambda i: (i, 0)),
            pl.BlockSpec((o1, tm), lambda i: (0, i)),
            pl.BlockSpec((tm, dv), lambda i: (i, 0)),
            pl.BlockSpec((tm, dv), lambda i: (i, 0)),
            pl.BlockSpec((tm, LANES), lambda i: (i, 0)),
            pl.BlockSpec((2 * nh, tm), lambda i: (0, i)),
        ],
        out_shape=[
            jax.ShapeDtypeStruct((m, o1), BF16),
            jax.ShapeDtypeStruct((o1, m), BF16),
            jax.ShapeDtypeStruct((m, dv), BF16),
            jax.ShapeDtypeStruct((m, dv), F32),
            jax.ShapeDtypeStruct((m, LANES), F32),
            jax.ShapeDtypeStruct((2 * nh, m), F32),
        ],
        compiler_params=_params(("parallel",), vmem),
        name="mlstm_proj",
    )(x, g_pre, wq, wkt, wv, wo, wgc, wgr, bc, br)


def _mlstm_kernel(q_ref, kt_ref, v_ref, og_ref, gc_ref, gr_ref, o_ref, st_ref, *, seq):
    nh, lc, dqk, dv = ML_HEADS, ML_CHUNK, ML_DQK, ML_DV
    st_ref[...] = jnp.zeros(st_ref.shape, F32)
    row = lax.broadcasted_iota(jnp.int32, (lc, lc), 0)
    col = lax.broadcasted_iota(jnp.int32, (lc, lc), 1)
    causal = col <= row
    tril = causal.astype(F32)
    triu = (row <= col).astype(F32)
    lane0 = lax.broadcasted_iota(jnp.int32, (lc, LANES), 1) == 0

    def body(j, ms):
        pos = pl.ds(pl.multiple_of(j * lc, lc), lc)
        gr_c = gr_ref[:, pos]
        gc_c = gc_ref[pos, :]
        b_rows = _dot_f32(gr_c, triu)
        b_cols = _dot_f32(tril, gc_c)
        new_ms = []
        for h in range(nh):
            m_h = ms[h]
            g_row = gr_c[h:h + 1, :] - b_rows[nh + h:nh + h + 1, :]
            b_col = b_cols[:, nh + h:nh + h + 1]
            g_col = gc_c[:, h:h + 1] - b_col
            dmat = jnp.where(causal, g_row, NEG_INF)
            mu = jnp.maximum(m_h, jnp.max(dmat, axis=1, keepdims=True))
            w = jnp.exp(dmat - mu)
            wi = jnp.exp(m_h - mu)
            emt = jnp.exp(-(b_col + mu))
            qh = q_ref[pos, h * dqk:(h + 1) * dqk]
            kth = kt_ref[h * dqk:(h + 1) * dqk, pos]
            vh = v_ref[pos, h * dv:(h + 1) * dv]
            a = _dot(qh, kth) * w
            state = st_ref[h]
            qc = _dot(qh, state.astype(BF16))
            num = _dot(a.astype(BF16), vh) + wi * qc[:, :dv]
            den = jnp.sum(a, axis=1, keepdims=True) + wi * qc[:, dv:dv + 1]
            hval = num / jnp.maximum(jnp.abs(den), emt)
            gate = jax.nn.sigmoid(og_ref[pos, h * dv:(h + 1) * dv])
            o_ref[pos, h * dv:(h + 1) * dv] = (hval * gate).astype(BF16)
            mu_last = mu[lc - 1:lc, :]
            we = jnp.exp(g_col - mu_last)
            decay = jnp.exp(m_h - mu_last)
            vaug = jnp.concatenate([vh.astype(F32) * we, jnp.where(lane0, we, 0.0)], axis=1)
            st_ref[h] = decay * state + _dot(kth, vaug.astype(BF16))
            new_ms.append(b_col[lc - 1:lc, :] + mu_last)
        return tuple(new_ms)

    lax.fori_loop(0, seq // lc, body, tuple(jnp.zeros((1, 1), F32) for _ in range(nh)))


def _mlstm_scan(q, kt, v, og, gc, gr, *, batch, seq):
    nh, dqk, dv = ML_HEADS, ML_DQK, ML_DV
    assert seq % ML_CHUNK == 0
    vmem = (2 * seq * (2 * nh * dqk * 2 + nh * dv * 2 + nh * dv * 4 + LANES * 4 + SUBLANES * 4 + nh * dv * 2)
            + nh * dqk * (dv + LANES) * 4 + (8 << 20))
    return pl.pallas_call(
        functools.partial(_mlstm_kernel, seq=seq),
        grid=(batch,),
        in_specs=[
            pl.BlockSpec((seq, nh * dqk), lambda b: (b, 0)),
            pl.BlockSpec((nh * dqk, seq), lambda b: (0, b)),
            pl.BlockSpec((seq, nh * dv), lambda b: (b, 0)),
            pl.BlockSpec((seq, nh * dv), lambda b: (b, 0)),
            pl.BlockSpec((seq, LANES), lambda b: (b, 0)),
            pl.BlockSpec((2 * nh, seq), lambda b: (0, b)),
        ],
        out_specs=pl.BlockSpec((seq, nh * dv), lambda b: (b, 0)),
        out_shape=jax.ShapeDtypeStruct((batch * seq, nh * dv), BF16),
        scratch_shapes=[pltpu.VMEM((nh, dqk, dv + LANES), F32)],
        compiler_params=_params(("parallel",), vmem),
        name="mlstm_scan",
    )(q, kt, v, og, gc, gr)


def kernel(x, norm_g, ffn1_wg, ffn1_wu, ffn1_wd, ffn2_wg, ffn2_wu, ffn2_wd,
           conv_w_in, conv_k, conv_w_out, attn_w_in, attn_w_out,
           mlstm_w_in, mlstm_b_i, mlstm_b_f, mlstm_w_out):
    batch, seq, d = x.shape
    depth = norm_g.shape[0]
    h = x.reshape(batch * seq, d)
    dils = tuple(dil for _, dil in DILATION_PAIRS)
    width = ATT_HEADS * ATT_HD
    for i in range(depth):
        g = [norm_g[i, k][None, :] for k in range(norm_g.shape[1])]
        h = _ffn(h, g[0], g[1], ffn1_wg, ffn1_wu, ffn1_wd, layer=i)
        mixer, j = i % N_MIXERS, i // N_MIXERS
        mix = None
        if mixer == 0:
            h = _conv_mixer(h, g[2], g[3], conv_w_in[j].astype(BF16), conv_k[j],
                            conv_w_out[j].astype(BF16), seq=seq)
        elif mixer == 1:
            w_in = attn_w_in[j].astype(BF16)
            qkvs = [_qkv_proj(h, g[2], w_in, group=gi, width=width, batch=batch, seq=seq, dil=dil)
                    for gi, dil in enumerate(dils)]
            att = _attention(qkvs, batch=batch, seq=seq, dils=dils, width=width)
            mix = (att.reshape(batch * seq, width), attn_w_out, j, g[3])
        else:
            q, kt, v, og, gc, gr = _mlstm_proj(h, g[2], mlstm_w_in[j], mlstm_b_i[j], mlstm_b_f[j])
            hg = _mlstm_scan(q, kt, v, og, gc, gr, batch=batch, seq=seq)
            mix = (hg, mlstm_w_out, j, g[3])
        h = _ffn(h, g[4], g[5], ffn2_wg, ffn2_wu, ffn2_wd, layer=i, mix=mix)
    return h.reshape(batch, seq, d)
```
